```python
import jax, jax.numpy as jnp
from jax import lax
import numpy as np

D_MODEL = 1024
BATCH = 4
SEQ = 8192
DEPTH = 1

MLA_HEADS = 8
QK_NOPE_DIM = 64
QK_ROPE_DIM = 32
QK_HEAD_DIM = QK_NOPE_DIM + QK_ROPE_DIM
V_HEAD_DIM = 64
Q_LORA_RANK = 384
KV_LORA_RANK = 256
ROPE_THETA = 10000.0
Q_BLOCK = 128
RWKV_HEADS = 8
RWKV_HEAD_DIM = 64
RWKV_DIM = RWKV_HEADS * RWKV_HEAD_DIM
DECAY_LORA = 64
AAA_LORA = 64
GATE_LORA = 128
GN_EPS = RWKV_HEAD_DIM * 1e-5
MLA_COLS = Q_LORA_RANK + KV_LORA_RANK + QK_ROPE_DIM
RWKV_COLS = 3 * RWKV_DIM + DECAY_LORA + AAA_LORA + GATE_LORA
GATE_COLS = 2 * D_MODEL
IN_COLS = MLA_COLS + RWKV_COLS + GATE_COLS
D_FF = 4 * D_MODEL
PLE_DIM = 256
RMS_EPS = 1e-6

kernel_name = "hybrid_mla_rwkv7_gated_block"


def rmsnorm(x, g):
    xf = x.astype(jnp.float32)
    y = xf * lax.rsqrt(jnp.mean(xf * xf, axis=-1, keepdims=True) + RMS_EPS)
    return (y * g.astype(jnp.float32)).astype(x.dtype)


def rope_tables(positions):
    half = QK_ROPE_DIM // 2
    inv_freq = ROPE_THETA ** (-jnp.arange(half, dtype=jnp.float32) / half)
    ang = positions.astype(jnp.float32)[..., None] * inv_freq
    return jnp.cos(ang), jnp.sin(ang)


def apply_rope(x, cos, sin):
    xf = x.astype(jnp.float32)
    x1, x2 = jnp.split(xf, 2, axis=-1)
    out = jnp.concatenate([x1 * cos - x2 * sin, x2 * cos + x1 * sin], axis=-1)
    return out.astype(x.dtype)


def causal_block_attention(q, k, v, scale):
    S = q.shape[1]
    outs = []
    for blk in range(S // Q_BLOCK):
        q0 = blk * Q_BLOCK
        kend = q0 + Q_BLOCK
        qb = q[:, q0:kend]
        s = jnp.einsum('bqhd,bkhd->bhqk', qb, k[:, :kend]).astype(jnp.float32) * scale
        mask = jnp.arange(kend)[None, :] <= (q0 + jnp.arange(Q_BLOCK))[:, None]
        s = jnp.where(mask, s, jnp.float32(-1e30))
        prob = jax.nn.softmax(s, axis=-1).astype(v.dtype)
        outs.append(jnp.einsum('bhqk,bkhd->bqhd', prob, v[:, :kend]))
    return jnp.concatenate(outs, axis=1)


def mla_branch(z, cos, sin, g_q_a, w_uq, g_kv_a, w_ukv, w_o_mla):
    B, S, _ = z.shape
    c_q, c_kv, k_pe = jnp.split(z, [Q_LORA_RANK, Q_LORA_RANK + KV_LORA_RANK], axis=-1)
    q = (rmsnorm(c_q, g_q_a) @ w_uq).reshape(B, S, MLA_HEADS, QK_HEAD_DIM)
    q_nope, q_pe = jnp.split(q, [QK_NOPE_DIM], axis=-1)
    q_pe = apply_rope(q_pe, cos[:, :, None, :], sin[:, :, None, :])
    kv = (rmsnorm(c_kv, g_kv_a) @ w_ukv).reshape(B, S, MLA_HEADS, QK_NOPE_DIM + V_HEAD_DIM)
    k_nope, v = jnp.split(kv, [QK_NOPE_DIM], axis=-1)
    k_pe = apply_rope(k_pe, cos, sin)
    qh = jnp.concatenate([q_nope, q_pe], axis=-1)
    kh = jnp.concatenate([k_nope, jnp.broadcast_to(k_pe[:, :, None, :], (B, S, MLA_HEADS, QK_ROPE_DIM))], axis=-1)
    o = causal_block_attention(qh, kh, v, QK_HEAD_DIM ** -0.5)
    return o.reshape(B, S, MLA_HEADS * V_HEAD_DIM) @ w_o_mla


def token_shift(z, mu):
    prev = jnp.pad(z, ((0, 0), (1, 0), (0, 0)))[:, :-1]
    return z + (prev - z) * mu


def wkv7_scan(r, decay, k, v, a_vec, b_vec):
    B, S, H, N = r.shape

    def step(state, inp):
        r_t, d_t, k_t, v_t, a_t, b_t = inp
        sa = jnp.einsum('bhvk,bhk->bhv', state, a_t)
        state = (state * d_t[:, :, None, :] + sa[..., None] * b_t[:, :, None, :]
                 + v_t[..., None] * k_t[:, :, None, :])
        return state, jnp.einsum('bhvk,bhk->bhv', state, r_t)

    xs = tuple(jnp.moveaxis(t, 1, 0) for t in (r, decay, k, v, a_vec, b_vec))
    _, ys = lax.scan(step, jnp.zeros((B, H, N, N), jnp.float32), xs)
    return jnp.moveaxis(ys, 0, 1)


def rwkv7_branch(z, mu_rwkv, w0, w2, a0, a2, g2, k_k, k_a, r_k, ln_x_w, ln_x_b, w_o_rwkv):
    B, S, _ = z.shape
    f32 = jnp.float32
    z = token_shift(z, mu_rwkv)
    r, k, v, w_lo, a_lo, g_lo = jnp.split(
        z, [RWKV_DIM, 2 * RWKV_DIM, 3 * RWKV_DIM, 3 * RWKV_DIM + DECAY_LORA,
            3 * RWKV_DIM + DECAY_LORA + AAA_LORA], axis=-1)
    w_log = -jax.nn.softplus(-(w0.astype(f32) + (jnp.tanh(w_lo) @ w2).astype(f32))) - 0.5
    decay = jnp.exp(-jnp.exp(w_log))
    a = jax.nn.sigmoid(a0.astype(f32) + (a_lo @ a2).astype(f32))
    g = jax.nn.sigmoid(g_lo) @ g2
    hs = (B, S, RWKV_HEADS, RWKV_HEAD_DIM)
    r = r.astype(f32).reshape(hs)
    k = k.astype(f32)
    v = v.astype(f32).reshape(hs)
    kk = (k * k_k.astype(f32)).reshape(hs)
    kk = kk / jnp.maximum(jnp.linalg.norm(kk, axis=-1, keepdims=True), 1e-12)
    k = (k * (1.0 + (a - 1.0) * k_a.astype(f32))).reshape(hs)
    a = a.reshape(hs)
    decay = decay.reshape(hs)
    y = wkv7_scan(r, decay, k, v, -kk, kk * a)
    mean = jnp.mean(y, axis=-1, keepdims=True)
    var = jnp.mean(jnp.square(y - mean), axis=-1, keepdims=True)
    y = ((y - mean) * lax.rsqrt(var + GN_EPS)).reshape(B, S, RWKV_DIM)
    y = y * ln_x_w.astype(f32) + ln_x_b.astype(f32)
    bonus = jnp.sum(r * k * r_k.astype(f32), axis=-1, keepdims=True) * v
    y = (y + bonus.reshape(B, S, RWKV_DIM)).astype(z.dtype) * g
    return y @ w_o_rwkv


def setup_inputs(seed: int = 0) -> dict:
    key = jax.random.key(seed)
    ks = jax.random.split(key, 40)
    f32 = jnp.float32
    L = DEPTH

    def nrm(k, shape, fan_in):
        return jax.random.normal(k, shape, f32) * (fan_in ** -0.5)

    def gain(k, shape):
        return 1.0 + 0.02 * jax.random.normal(k, shape, f32)

    H, N = RWKV_HEADS, RWKV_HEAD_DIM
    return {
        "x": jax.random.normal(ks[0], (BATCH, SEQ, D_MODEL), f32),
        "p": jax.random.normal(ks[1], (DEPTH, BATCH, SEQ, PLE_DIM), f32),
        "positions": (jax.random.randint(ks[2], (BATCH, 1), 0, 1024, jnp.int32)
                      + jnp.arange(SEQ, dtype=jnp.int32)[None, :]),
        "g_mix": gain(ks[3], (L, D_MODEL)),
        "w_in": nrm(ks[4], (L, D_MODEL, IN_COLS), D_MODEL),
        "g_q_a": gain(ks[5], (L, Q_LORA_RANK)),
        "w_uq": nrm(ks[6], (L, Q_LORA_RANK, MLA_HEADS * QK_HEAD_DIM), Q_LORA_RANK),
        "g_kv_a": gain(ks[7], (L, KV_LORA_RANK)),
        "w_ukv": nrm(ks[8], (L, KV_LORA_RANK, MLA_HEADS * (QK_NOPE_DIM + V_HEAD_DIM)), KV_LORA_RANK),
        "w_o_mla": nrm(ks[9], (L, MLA_HEADS * V_HEAD_DIM, D_MODEL), MLA_HEADS * V_HEAD_DIM),
        "mu_rwkv": jax.random.uniform(ks[10], (L, RWKV_COLS), f32),
        "w0": jax.random.uniform(ks[11], (L, RWKV_DIM), f32, -6.5, -1.5),
        "w2": nrm(ks[12], (L, DECAY_LORA, RWKV_DIM), DECAY_LORA),
        "a0": 0.1 * jax.random.normal(ks[13], (L, RWKV_DIM), f32),
        "a2": nrm(ks[14], (L, AAA_LORA, RWKV_DIM), AAA_LORA),
        "g2": nrm(ks[15], (L, GATE_LORA, RWKV_DIM), GATE_LORA),
        "k_k": 0.85 + 0.02 * jax.random.normal(ks[16], (L, RWKV_DIM), f32),
        "k_a": gain(ks[17], (L, RWKV_DIM)),
        "r_k": 0.1 * jax.random.normal(ks[18], (L, H, N), f32),
        "ln_x_w": gain(ks[19], (L, RWKV_DIM)),
        "ln_x_b": 0.02 * jax.random.normal(ks[20], (L, RWKV_DIM), f32),
        "w_o_rwkv": nrm(ks[21], (L, RWKV_DIM, D_MODEL), RWKV_DIM),
        "w_out": nrm(ks[22], (L, D_MODEL, D_MODEL), D_MODEL),
        "g_ffn": gain(ks[23], (L, D_MODEL)),
        "w_ffn_up": nrm(ks[24], (L, D_MODEL, D_FF), D_MODEL),
        "w_ffn_down": nrm(ks[25], (L, D_FF, D_MODEL), D_FF),
        "g_ple": gain(ks[26], (L, D_MODEL)),
        "w_ple_gate": nrm(ks[27], (L, D_MODEL, D_MODEL), D_MODEL),
        "w_ple_proj": nrm(ks[28], (L, PLE_DIM, D_MODEL), PLE_DIM),
        "g_final": gain(ks[29], (D_MODEL,)),
    }


def reference(x, p, positions, g_mix, w_in, g_q_a, w_uq, g_kv_a, w_ukv, w_o_mla,
              mu_rwkv, w0, w2, a0, a2, g2, k_k, k_a, r_k, ln_x_w, ln_x_b, w_o_rwkv,
              w_out, g_ffn, w_ffn_up, w_ffn_down, g_ple, w_ple_gate, w_ple_proj, g_final):
    cos, sin = rope_tables(positions)
    for i in range(DEPTH):
        h = rmsnorm(x, g_mix[i])
        z = h @ w_in[i]
        z_mla, z_rwkv, z_gate = jnp.split(z, [MLA_COLS, MLA_COLS + RWKV_COLS], axis=-1)
        y_a = mla_branch(z_mla, cos, sin, g_q_a[i], w_uq[i], g_kv_a[i], w_ukv[i], w_o_mla[i])
        y_b = rwkv7_branch(z_rwkv, mu_rwkv[i], w0[i], w2[i], a0[i], a2[i], g2[i], k_k[i], k_a[i],
                           r_k[i], ln_x_w[i], ln_x_b[i], w_o_rwkv[i])
        gate = jax.nn.sigmoid(z_gate)
        gate_a, gate_b = jnp.split(gate, 2, axis=-1)
        x = x + (gate_a * y_a + gate_b * y_b) @ w_out[i]
        h = rmsnorm(x, g_ffn[i])
        x = x + jnp.square(jax.nn.relu(h @ w_ffn_up[i])) @ w_ffn_down[i]
        ple_gate = jax.nn.sigmoid(rmsnorm(x, g_ple[i]) @ w_ple_gate[i])
        x = x + ple_gate * (p[i] @ w_ple_proj[i])
    return rmsnorm(x, g_final)
```

```python
import functools

import jax
import jax.numpy as jnp
from jax import lax
from jax.experimental import pallas as pl
from jax.experimental.pallas import tpu as pltpu

D_MODEL = 1024
MLA_HEADS = 8
QK_NOPE_DIM = 64
QK_ROPE_DIM = 32
QK_HEAD_DIM = QK_NOPE_DIM + QK_ROPE_DIM
V_HEAD_DIM = 64
Q_LORA_RANK = 384
KV_LORA_RANK = 256
ROPE_THETA = 10000.0
RWKV_HEADS = 8
RWKV_HEAD_DIM = 64
RWKV_DIM = RWKV_HEADS * RWKV_HEAD_DIM
DECAY_LORA = 64
AAA_LORA = 64
GATE_LORA = 128
GN_EPS = RWKV_HEAD_DIM * 1e-5
MLA_COLS = Q_LORA_RANK + KV_LORA_RANK + QK_ROPE_DIM
RWKV_COLS = 3 * RWKV_DIM + DECAY_LORA + AAA_LORA + GATE_LORA
D_FF = 4 * D_MODEL
PLE_DIM = 256
RMS_EPS = 1e-6

LANES = 128
HEAD_SLAB = LANES
VMEM_LIMIT = 56 * 1024 * 1024

TM_A = 256
TQ = 256
WKV_CHUNK = 64
WKV_BLOCK = 256
TM_D = 256
FF_CHUNK = 1024

A_CQ = 0
A_CKV = A_CQ + Q_LORA_RANK
A_KPE = A_CKV + KV_LORA_RANK
A_KPE_ROT = A_KPE + HEAD_SLAB
A_RWKV = A_KPE_ROT + HEAD_SLAB
A_COLS = A_RWKV + RWKV_COLS

_BF16 = jnp.bfloat16
_F32 = jnp.float32


def _dot(a, b):
    return jnp.dot(a, b, preferred_element_type=_F32)


def _dot_nt(a, b):
    return lax.dot_general(a, b, (((1,), (1,)), ((), ())), preferred_element_type=_F32)


def _dot_tn(a, b):
    return lax.dot_general(a, b, (((0,), (0,)), ((), ())), preferred_element_type=_F32)


def _rms(x, g):
    return x * lax.rsqrt(jnp.mean(x * x, axis=-1, keepdims=True) + RMS_EPS) * g


def _resident(shape):
    zeros = (0,) * len(shape)
    return pl.BlockSpec(shape, lambda *_: zeros, pipeline_mode=pl.Buffered(1))


def _rope_kernel(pos_ref, inv_ref, cos_ref, sin_ref):
    ang = pos_ref[...] * inv_ref[...]
    cos_ref[...] = jnp.cos(ang)
    sin_ref[...] = jnp.sin(ang)


def _rope_tables(positions):
    half = QK_ROPE_DIM // 2
    t = positions.size
    inv_freq = ROPE_THETA ** (-jnp.arange(half, dtype=_F32) / half)
    pos = jnp.repeat(positions.reshape(-1).astype(_F32), half).reshape(t * half // LANES, LANES)
    inv = jnp.tile(inv_freq, LANES // half).reshape(1, LANES)
    cos, sin = pl.pallas_call(
        _rope_kernel,
        out_shape=(jax.ShapeDtypeStruct(pos.shape, _F32),) * 2,
        name="rope_tables",
    )(pos, inv)
    return cos.reshape(t, half), sin.reshape(t, half)


def _stage_a_kernel(x_ref, cosq_ref, sinq_ref, gmix_ref, win_ref, gq_ref, wq_ref, wqrot_ref,
                    gkv_ref, wk_ref, wv_ref, mu_ref, w0_ref, wwa_ref, a0_ref, g2_ref,
                    kk_ref, ka_ref,
                    q_out, k_out, v_out, r_out, ld_out, kmod_out, vr_out, kkn_out, b_out, g_out,
                    carry_ref):
    tm = x_ref.shape[1]

    @pl.when(pl.program_id(1) == 0)
    def _():
        carry_ref[...] = jnp.zeros_like(carry_ref)

    h = _rms(x_ref[0], gmix_ref[...]).astype(_BF16)
    z = _dot(h, win_ref[...])

    cosq = jnp.concatenate([cosq_ref[...]] * MLA_HEADS, axis=1)
    sinq = jnp.concatenate([sinq_ref[...]] * MLA_HEADS, axis=1)
    cq = _rms(z[:, A_CQ:A_CQ + Q_LORA_RANK], gq_ref[...]).astype(_BF16)
    q = (_dot(cq, wq_ref[...]) * cosq + _dot(cq, wqrot_ref[...]) * sinq) * (QK_HEAD_DIM ** -0.5)
    q_out[0] = q.astype(_BF16)
    ckv = _rms(z[:, A_CKV:A_CKV + KV_LORA_RANK], gkv_ref[...]).astype(_BF16)
    kpe = (z[:, A_KPE:A_KPE + HEAD_SLAB] * cosq_ref[...]
           + z[:, A_KPE_ROT:A_KPE_ROT + HEAD_SLAB] * sinq_ref[...])
    k = _dot(ckv, wk_ref[...]) + jnp.concatenate([kpe] * MLA_HEADS, axis=1)
    k_out[0] = k.astype(_BF16)
    v_out[0] = _dot(ckv, wv_ref[...]).astype(_BF16)

    zr = z[:, A_RWKV:A_RWKV + RWKV_COLS]
    row = lax.broadcasted_iota(jnp.int32, zr.shape, 0)
    prev = jnp.where(row == 0, carry_ref[...], pltpu.roll(zr, shift=1, axis=0))
    carry_ref[...] = zr[tm - 1:tm, :]
    zs = zr + (prev - zr) * mu_ref[...]
    r = zs[:, 0:RWKV_DIM]
    k_raw = zs[:, RWKV_DIM:2 * RWKV_DIM]
    v_r = zs[:, 2 * RWKV_DIM:3 * RWKV_DIM]
    lo = zs[:, 3 * RWKV_DIM:3 * RWKV_DIM + LANES]
    g_lo = zs[:, 3 * RWKV_DIM + LANES:]
    lane = lax.broadcasted_iota(jnp.int32, lo.shape, 1)
    lo = jnp.where(lane < DECAY_LORA, jnp.tanh(lo), lo).astype(_BF16)
    wa = _dot(lo, wwa_ref[...])
    w_pre = w0_ref[...] + wa[:, :RWKV_DIM]
    neg = -w_pre
    softplus = jnp.maximum(neg, 0.0) + jnp.log(1.0 + jnp.exp(-jnp.abs(neg)))
    log_decay = -jnp.exp(-softplus - 0.5)
    a = jax.nn.sigmoid(a0_ref[...] + wa[:, RWKV_DIM:])
    g = _dot(jax.nn.sigmoid(g_lo).astype(_BF16), g2_ref[...])
    kk = k_raw * kk_ref[...]
    k_mod = k_raw * (1.0 + (a - 1.0) * ka_ref[...])
    for hd in range(RWKV_HEADS):
        sl = slice(hd * RWKV_HEAD_DIM, (hd + 1) * RWKV_HEAD_DIM)
        kk_h = kk[:, sl]
        norm = jnp.sqrt(jnp.sum(kk_h * kk_h, axis=-1, keepdims=True))
        kkn_h = kk_h / jnp.maximum(norm, 1e-12)
        r_out[0, hd] = r[:, sl]
        ld_out[0, hd] = log_decay[:, sl]
        kmod_out[0, hd] = k_mod[:, sl]
        vr_out[0, hd] = v_r[:, sl]
        kkn_out[0, hd] = kkn_h
        b_out[0, hd] = kkn_h * a[:, sl]
        g_out[0, hd] = g[:, sl]


def _stage_a(x, cosq, sinq, gmix, win, gq, wq, wqrot, gkv, wk, wv, mu, w0, wwa, a0, g2, kk, ka):
    b, s, _ = x.shape
    tm = min(TM_A, s)
    tok = lambda bi, si: (bi, si, 0)
    head_major = jax.ShapeDtypeStruct((b, RWKV_HEADS, s, RWKV_HEAD_DIM), _F32)
    hm_spec = pl.BlockSpec((1, RWKV_HEADS, tm, RWKV_HEAD_DIM), lambda bi, si: (bi, 0, si, 0))
    weights = (gmix, win, gq, wq, wqrot, gkv, wk, wv, mu, w0, wwa, a0, g2, kk, ka)
    return pl.pallas_call(
        _stage_a_kernel,
        grid=(b, s // tm),
        in_specs=[pl.BlockSpec((1, tm, D_MODEL), tok),
                  pl.BlockSpec((tm, HEAD_SLAB), lambda bi, si: (bi * (s // tm) + si, 0)),
                  pl.BlockSpec((tm, HEAD_SLAB), lambda bi, si: (bi * (s // tm) + si, 0))]
                 + [_resident(w.shape) for w in weights],
        out_specs=[pl.BlockSpec((1, tm, MLA_HEADS * HEAD_SLAB), tok),
                   pl.BlockSpec((1, tm, MLA_HEADS * HEAD_SLAB), tok),
                   pl.BlockSpec((1, tm, MLA_HEADS * V_HEAD_DIM), tok)] + [hm_spec] * 7,
        out_shape=[jax.ShapeDtypeStruct((b, s, MLA_HEADS * HEAD_SLAB), _BF16),
                   jax.ShapeDtypeStruct((b, s, MLA_HEADS * HEAD_SLAB), _BF16),
                   jax.ShapeDtypeStruct((b, s, MLA_HEADS * V_HEAD_DIM), _BF16)] + [head_major] * 7,
        scratch_shapes=[pltpu.VMEM((1, RWKV_COLS), _F32)],
        compiler_params=pltpu.CompilerParams(
            dimension_semantics=("arbitrary", "arbitrary"), vmem_limit_bytes=VMEM_LIMIT),
        name="stage_a",
    )(x, cosq, sinq, *weights)


def _attn_kernel(q_ref, k_ref, v_ref, o_ref):
    tq = q_ref.shape[1]
    qi = pl.program_id(2)
    q = q_ref[0]
    row = lax.broadcasted_iota(jnp.int32, (tq, tq), 0)
    col = lax.broadcasted_iota(jnp.int32, (tq, tq), 1)
    causal = col <= row
    outs = []
    for hh in range(2):
        qh = q[:, hh * HEAD_SLAB:(hh + 1) * HEAD_SLAB]

        def step(kt, carry, masked, hh=hh, qh=qh):
            m, l, acc = carry
            off = pl.multiple_of(kt * tq, tq)
            kb = k_ref[0, pl.ds(off, tq), hh * HEAD_SLAB:(hh + 1) * HEAD_SLAB]
            vb = v_ref[0, pl.ds(off, tq), :]
            s = _dot_nt(qh, kb)
            if masked:
                s = jnp.where(causal, s, -1e30)
            m_new = jnp.maximum(m, jnp.max(s, axis=-1, keepdims=True))
            alpha = jnp.exp(m - m_new)
            p = jnp.exp(s - m_new)
            l = alpha * l + jnp.sum(p, axis=-1, keepdims=True)
            acc = alpha * acc + _dot(p.astype(_BF16), vb)
            return m_new, l, acc

        init = (jnp.full((tq, 1), -1e30, _F32), jnp.zeros((tq, 1), _F32),
                jnp.zeros((tq, 2 * V_HEAD_DIM), _F32))
        carry = lax.fori_loop(0, qi, functools.partial(step, masked=False), init)
        _, l, acc = step(qi, carry, masked=True)
        outs.append(acc / l)
    lane = lax.broadcasted_iota(jnp.int32, outs[0].shape, 1)
    o_ref[0] = jnp.where(lane < V_HEAD_DIM, outs[0], outs[1]).astype(_BF16)


def _attention(q, k, v):
    b, s, _ = q.shape
    tq = min(TQ, s)
    pairs = MLA_HEADS // 2
    return pl.pallas_call(
        _attn_kernel,
        grid=(b, pairs, s // tq),
        in_specs=[pl.BlockSpec((1, tq, 2 * HEAD_SLAB), lambda bi, hp, qi: (bi, qi, hp)),
                  pl.BlockSpec((1, s, 2 * HEAD_SLAB), lambda bi, hp, qi: (bi, 0, hp)),
                  pl.BlockSpec((1, s, 2 * V_HEAD_DIM), lambda bi, hp, qi: (bi, 0, hp))],
        out_specs=pl.BlockSpec((1, tq, 2 * V_HEAD_DIM), lambda bi, hp, qi: (bi, qi, hp)),
        out_shape=jax.ShapeDtypeStruct((b, s, MLA_HEADS * V_HEAD_DIM), _BF16),
        compiler_params=pltpu.CompilerParams(
            dimension_semantics=("parallel", "parallel", "arbitrary"),
            vmem_limit_bytes=VMEM_LIMIT),
        name="mla_attention",
    )(q, k, v)


def _wkv_head_chunk(r, ld, k, v, kk, b, state):
    c = r.shape[0]
    row = lax.broadcasted_iota(jnp.int32, (c, c), 0)
    col = lax.broadcasted_iota(jnp.int32, (c, c), 1)
    tri = (col <= row).astype(_BF16)
    ld_hi = ld.astype(_BF16)
    ld_lo = (ld - ld_hi.astype(_F32)).astype(_BF16)
    cum = _dot(tri, ld_hi) + _dot(tri, ld_lo)
    cum_last = cum[c - 1:c, :]
    e_neg = jnp.exp(-cum)
    e_tail = jnp.exp(cum_last - cum)
    r_hat = r * jnp.exp(cum)
    a_hat = -kk * jnp.exp(cum - ld)
    lhs = jnp.concatenate([a_hat, r_hat], axis=0).astype(_BF16)
    v_bf = v.astype(_BF16)
    a_b = _dot_nt(lhs, (b * e_neg).astype(_BF16))
    a_k = _dot_nt(lhs, (k * e_neg).astype(_BF16))
    strict = col < row
    incl = col <= row
    a_ab = jnp.where(strict, a_b[:c], 0.0)
    a_rb = jnp.where(incl, a_b[c:], 0.0)
    a_akrk = jnp.concatenate([jnp.where(strict, a_k[:c], 0.0), jnp.where(incl, a_k[c:], 0.0)], axis=0)

    t = jnp.where(row == col, 1.0, 0.0) + jnp.where((row ^ col) == 1, a_ab, 0.0)
    size = 2
    while size < c:
        level = ((row ^ col) >= size) & ((row ^ col) < 2 * size)
        a_l = jnp.where(level, a_ab, 0.0).astype(_BF16)
        t_bf = t.astype(_BF16)
        t = t + _dot(_dot(t_bf, a_l).astype(_BF16), t_bf)
        size *= 2

    s_part = _dot_nt(lhs, state.astype(_BF16))
    v_part = _dot(a_akrk.astype(_BF16), v_bf)
    x = s_part[:c] + v_part[:c]
    u = _dot(t.astype(_BF16), x.astype(_BF16))
    u_bf = u.astype(_BF16)
    y = s_part[c:] + v_part[c:] + _dot(a_rb.astype(_BF16), u_bf)
    vu = jnp.concatenate([v_bf, u_bf], axis=0)
    kb = jnp.concatenate([k * e_tail, b * e_tail], axis=0).astype(_BF16)
    new_state = state * jnp.exp(cum_last) + _dot_tn(vu, kb)
    return y, new_state


def _wkv_kernel(r_ref, ld_ref, k_ref, v_ref, kk_ref, b_ref, g_ref, rk_ref, lnw_ref, lnb_ref,
                o_ref, state_ref, y_ref):
    blk = r_ref.shape[2]
    c = WKV_CHUNK

    @pl.when(pl.program_id(1) == 0)
    def _():
        state_ref[...] = jnp.zeros_like(state_ref)

    def chunk(ci, _):
        sl = pl.ds(pl.multiple_of(ci * c, c), c)
        for hd in range(RWKV_HEADS):
            y, new_state = _wkv_head_chunk(r_ref[0, hd, sl, :], ld_ref[0, hd, sl, :],
                                           k_ref[0, hd, sl, :], v_ref[0, hd, sl, :],
                                           kk_ref[0, hd, sl, :], b_ref[0, hd, sl, :],
                                           state_ref[hd])
            state_ref[hd] = new_state
            y_ref[hd, sl, :] = y
        return 0

    lax.fori_loop(0, blk // c, chunk, 0)

    outs = []
    for hd in range(RWKV_HEADS):
        y = y_ref[hd]
        mean = jnp.mean(y, axis=-1, keepdims=True)
        var = jnp.mean(jnp.square(y - mean), axis=-1, keepdims=True)
        yn = (y - mean) * lax.rsqrt(var + GN_EPS) * lnw_ref[hd:hd + 1, :] + lnb_ref[hd:hd + 1, :]
        bonus = jnp.sum(r_ref[0, hd] * k_ref[0, hd] * rk_ref[hd:hd + 1, :], axis=-1,
                        keepdims=True) * v_ref[0, hd]
        outs.append((yn + bonus) * g_ref[0, hd])
    o_ref[0] = jnp.concatenate(outs, axis=-1).astype(_BF16)


def _wkv(r, ld, k, v, kk, bvec, g, rk, lnw, lnb):
    b, nh, s, n = r.shape
    blk = min(WKV_BLOCK, s)
    hm_spec = pl.BlockSpec((1, nh, blk, n), lambda bi, si: (bi, 0, si, 0))
    return pl.pallas_call(
        _wkv_kernel,
        grid=(b, s // blk),
        in_specs=[hm_spec] * 7 + [_resident(rk.shape), _resident(lnw.shape), _resident(lnb.shape)],
        out_specs=pl.BlockSpec((1, blk, nh * n), lambda bi, si: (bi, si, 0)),
        out_shape=jax.ShapeDtypeStruct((b, s, nh * n), _BF16),
        scratch_shapes=[pltpu.VMEM((nh, n, n), _F32), pltpu.VMEM((nh, blk, n), _F32)],
        compiler_params=pltpu.CompilerParams(
            dimension_semantics=("parallel", "arbitrary"), vmem_limit_bytes=VMEM_LIMIT),
        name="wkv7",
    )(r, ld, k, v, kk, bvec, g, rk, lnw, lnb)


def _stage_d_kernel(x_ref, oa_ref, ob_ref, p_ref, gmix_ref, wgate_ref, woa_ref, wob_ref, wout_ref,
                    gffn_ref, wup_ref, wdown_ref, gple_ref, wpg_ref, wpp_ref, gfin_ref, out_ref, *,
                    final_norm):
    x = x_ref[...]
    h = _rms(x, gmix_ref[...]).astype(_BF16)
    gate = jax.nn.sigmoid(_dot(h, wgate_ref[...]))
    y_a = _dot(oa_ref[...], woa_ref[...])
    y_b = _dot(ob_ref[...], wob_ref[...])
    mix = gate[:, :D_MODEL] * y_a + gate[:, D_MODEL:] * y_b
    x = x + _dot(mix.astype(_BF16), wout_ref[...])
    h = _rms(x, gffn_ref[...]).astype(_BF16)
    for c0 in range(0, D_FF, FF_CHUNK):
        hid = jnp.square(jnp.maximum(_dot(h, wup_ref[:, c0:c0 + FF_CHUNK]), 0.0))
        x = x + _dot(hid.astype(_BF16), wdown_ref[c0:c0 + FF_CHUNK, :])
    ple_gate = jax.nn.sigmoid(_dot(_rms(x, gple_ref[...]).astype(_BF16), wpg_ref[...]))
    x = x + ple_gate * _dot(p_ref[...].astype(_BF16), wpp_ref[...])
    out_ref[...] = _rms(x, gfin_ref[...]) if final_norm else x


def _stage_d(x, oa, ob, p, gmix, wgate, woa, wob, wout, gffn, wup, wdown, gple, wpg, wpp, gfin,
             final_norm):
    t = x.shape[0]
    tm = min(TM_D, t)
    tok = lambda i: (i, 0)
    weights = (gmix, wgate, woa, wob, wout, gffn, wup, wdown, gple, wpg, wpp, gfin)
    return pl.pallas_call(
        functools.partial(_stage_d_kernel, final_norm=final_norm),
        grid=(t // tm,),
        in_specs=[pl.BlockSpec((tm, D_MODEL), tok),
                  pl.BlockSpec((tm, oa.shape[1]), tok),
                  pl.BlockSpec((tm, ob.shape[1]), tok),
                  pl.BlockSpec((tm, PLE_DIM), tok)] + [_resident(w.shape) for w in weights],
        out_specs=pl.BlockSpec((tm, D_MODEL), tok),
        out_shape=jax.ShapeDtypeStruct((t, D_MODEL), _F32),
        compiler_params=pltpu.CompilerParams(
            dimension_semantics=("parallel",), vmem_limit_bytes=VMEM_LIMIT),
        name="stage_d",
    )(x, oa, ob, p, *weights)


def _head_slabs(w, width, pieces):
    cols = []
    for hd in range(MLA_HEADS):
        used = 0
        for piece in pieces:
            if isinstance(piece, int):
                cols.append(jnp.zeros((w.shape[0], piece), w.dtype))
                used += piece
            else:
                start, stop, sign = piece
                cols.append(sign * w[:, hd * width + start:hd * width + stop])
                used += stop - start
        assert used == HEAD_SLAB
    return jnp.concatenate(cols, axis=1)


def _prepare_layer(i, g_mix, w_in, g_q_a, w_uq, g_kv_a, w_ukv, w_o_mla, mu_rwkv, w0, w2, a0, a2,
                   g2, k_k, k_a, r_k, ln_x_w, ln_x_b, w_o_rwkv, w_out, g_ffn, w_ffn_up, w_ffn_down,
                   g_ple, w_ple_gate, w_ple_proj):
    half = QK_ROPE_DIM // 2
    row = lambda a: a[i].reshape(1, -1)
    w = w_in[i]
    d = w.shape[0]
    kpe = w[:, Q_LORA_RANK + KV_LORA_RANK:MLA_COLS]
    zeros = lambda n: jnp.zeros((d, n), w.dtype)
    kpe_slab = jnp.concatenate([zeros(QK_NOPE_DIM), kpe, zeros(HEAD_SLAB - QK_HEAD_DIM)], axis=1)
    kpe_rot = jnp.concatenate([zeros(QK_NOPE_DIM), -kpe[:, half:], kpe[:, :half],
                               zeros(HEAD_SLAB - QK_HEAD_DIM)], axis=1)
    win = jnp.concatenate([w[:, :Q_LORA_RANK + KV_LORA_RANK], kpe_slab, kpe_rot,
                           w[:, MLA_COLS:MLA_COLS + RWKV_COLS]], axis=1).astype(_BF16)
    wgate = w[:, MLA_COLS + RWKV_COLS:].astype(_BF16)
    pad = HEAD_SLAB - QK_HEAD_DIM
    wq = _head_slabs(w_uq[i], QK_HEAD_DIM, [(0, QK_HEAD_DIM, 1.0), pad]).astype(_BF16)
    wqrot = _head_slabs(w_uq[i], QK_HEAD_DIM,
                        [QK_NOPE_DIM, (QK_NOPE_DIM + half, QK_HEAD_DIM, -1.0),
                         (QK_NOPE_DIM, QK_NOPE_DIM + half, 1.0), pad]).astype(_BF16)
    kv_width = QK_NOPE_DIM + V_HEAD_DIM
    wk = _head_slabs(w_ukv[i], kv_width, [(0, QK_NOPE_DIM, 1.0), HEAD_SLAB - QK_NOPE_DIM]).astype(_BF16)
    wv = jnp.concatenate([w_ukv[i][:, hd * kv_width + QK_NOPE_DIM:(hd + 1) * kv_width]
                          for hd in range(MLA_HEADS)], axis=1).astype(_BF16)
    wwa = jnp.concatenate(
        [jnp.concatenate([w2[i], jnp.zeros_like(w2[i])], axis=1),
         jnp.concatenate([jnp.zeros_like(a2[i]), a2[i]], axis=1)], axis=0).astype(_BF16)
    stage_a = (row(g_mix), win, row(g_q_a), wq, wqrot, row(g_kv_a), wk, wv, row(mu_rwkv), row(w0),
               wwa, row(a0), g2[i].astype(_BF16), row(k_k), row(k_a))
    wkv = (r_k[i], ln_x_w[i].reshape(RWKV_HEADS, RWKV_HEAD_DIM),
           ln_x_b[i].reshape(RWKV_HEADS, RWKV_HEAD_DIM))
    stage_d = (row(g_mix), wgate, w_o_mla[i].astype(_BF16), w_o_rwkv[i].astype(_BF16),
               w_out[i].astype(_BF16), row(g_ffn), w_ffn_up[i].astype(_BF16),
               w_ffn_down[i].astype(_BF16), row(g_ple), w_ple_gate[i].astype(_BF16),
               w_ple_proj[i].astype(_BF16))
    return stage_a, wkv, stage_d


def kernel(x, p, positions, g_mix, w_in, g_q_a, w_uq, g_kv_a, w_ukv, w_o_mla, mu_rwkv, w0, w2, a0, a2, g2, k_k, k_a, r_k, ln_x_w, ln_x_b, w_o_rwkv, w_out, g_ffn, w_ffn_up, w_ffn_down, g_ple, w_ple_gate, w_ple_proj, g_final):
    b, s, d = x.shape
    depth = w_in.shape[0]
    t = b * s
    cos, sin = _rope_tables(positions)
    ones = jnp.ones((t, QK_NOPE_DIM), _F32)
    zeros = jnp.zeros((t, QK_NOPE_DIM), _F32)
    pad = jnp.zeros((t, HEAD_SLAB - QK_HEAD_DIM), _F32)
    cosq = jnp.concatenate([ones, cos, cos, pad], axis=1)
    sinq = jnp.concatenate([zeros, sin, sin, pad], axis=1)
    gfin = g_final.reshape(1, -1)
    for i in range(depth):
        sa, sw, sd = _prepare_layer(i, g_mix, w_in, g_q_a, w_uq, g_kv_a, w_ukv, w_o_mla, mu_rwkv,
                                    w0, w2, a0, a2, g2, k_k, k_a, r_k, ln_x_w, ln_x_b, w_o_rwkv,
                                    w_out, g_ffn, w_ffn_up, w_ffn_down, g_ple, w_ple_gate,
                                    w_ple_proj)
        q, k, v, r, ld, kmod, vr, kkn, bvec, g = _stage_a(x, cosq, sinq, *sa)
        o_a = _attention(q, k, v)
        o_b = _wkv(r, ld, kmod, vr, kkn, bvec, g, *sw)
        out = _stage_d(x.reshape(t, d), o_a.reshape(t, -1), o_b.reshape(t, -1), p[i].reshape(t, -1),
                       *sd, gfin, final_norm=(i == depth - 1))
        x = out.reshape(b, s, d)
    return x
```

```python
import functools

import jax
import jax.numpy as jnp
from jax import lax
from jax.experimental import pallas as pl
from jax.experimental.pallas import tpu as pltpu

D_MODEL = 1024
MLA_HEADS = 8
QK_NOPE_DIM = 64
QK_ROPE_DIM = 32
QK_HEAD_DIM = QK_NOPE_DIM + QK_ROPE_DIM
V_HEAD_DIM = 64
Q_LORA_RANK = 384
KV_LORA_RANK = 256
ROPE_THETA = 10000.0
RWKV_HEADS = 8
RWKV_HEAD_DIM = 64
RWKV_DIM = RWKV_HEADS * RWKV_HEAD_DIM
DECAY_LORA = 64
AAA_LORA = 64
GATE_LORA = 128
GN_EPS = RWKV_HEAD_DIM * 1e-5
MLA_COLS = Q_LORA_RANK + KV_LORA_RANK + QK_ROPE_DIM
RWKV_COLS = 3 * RWKV_DIM + DECAY_LORA + AAA_LORA + GATE_LORA
D_FF = 4 * D_MODEL
PLE_DIM = 256
RMS_EPS = 1e-6

LANES = 128
HEAD_SLAB = LANES
VMEM_LIMIT = 56 * 1024 * 1024

TM_A = 256
TQ = 256
ATTN_HEADS = 8
ATTN_SUM_ROWS = 16
WKV_CHUNK = 64
WKV_BLOCK = 256
TM_D = 256
FF_CHUNK = 1024

A_CQ = 0
A_CKV = A_CQ + Q_LORA_RANK
A_KPE = A_CKV + KV_LORA_RANK
A_KPE_ROT = A_KPE + HEAD_SLAB
A_RWKV = A_KPE_ROT + HEAD_SLAB
A_COLS = A_RWKV + RWKV_COLS

_BF16 = jnp.bfloat16
_F32 = jnp.float32


def _dot(a, b):
    return jnp.dot(a, b, preferred_element_type=_F32)


def _dot_nt(a, b):
    return lax.dot_general(a, b, (((1,), (1,)), ((), ())), preferred_element_type=_F32)


def _dot_tn(a, b):
    return lax.dot_general(a, b, (((0,), (0,)), ((), ())), preferred_element_type=_F32)


def _rms(x, g):
    return x * lax.rsqrt(jnp.mean(x * x, axis=-1, keepdims=True) + RMS_EPS) * g


def _resident(shape):
    zeros = (0,) * len(shape)
    return pl.BlockSpec(shape, lambda *_: zeros, pipeline_mode=pl.Buffered(1))


def _rope_kernel(pos_ref, inv_ref, cos_ref, sin_ref):
    ang = pos_ref[...] * inv_ref[...]
    cos_ref[...] = jnp.cos(ang)
    sin_ref[...] = jnp.sin(ang)


def _rope_tables(positions):
    half = QK_ROPE_DIM // 2
    t = positions.size
    inv_freq = ROPE_THETA ** (-jnp.arange(half, dtype=_F32) / half)
    pos = jnp.repeat(positions.reshape(-1).astype(_F32), half).reshape(t * half // LANES, LANES)
    inv = jnp.tile(inv_freq, LANES // half).reshape(1, LANES)
    cos, sin = pl.pallas_call(
        _rope_kernel,
        out_shape=(jax.ShapeDtypeStruct(pos.shape, _F32),) * 2,
        name="rope_tables",
    )(pos, inv)
    return cos.reshape(t, half), sin.reshape(t, half)


def _stage_a_kernel(x_ref, cosq_ref, sinq_ref, gmix_ref, win_ref, gq_ref, wq_ref, wqrot_ref,
                    gkv_ref, wk_ref, wv_ref, mu_ref, w0_ref, wwa_ref, a0_ref, g2_ref,
                    kk_ref, ka_ref,
                    q_out, k_out, v_out, r_out, ld_out, kmod_out, vr_out, kkn_out, b_out, g_out,
                    carry_ref):
    tm = x_ref.shape[1]

    @pl.when(pl.program_id(1) == 0)
    def _():
        carry_ref[...] = jnp.zeros_like(carry_ref)

    h = _rms(x_ref[0], gmix_ref[...]).astype(_BF16)
    z = _dot(h, win_ref[...])

    cosq = jnp.concatenate([cosq_ref[...]] * MLA_HEADS, axis=1)
    sinq = jnp.concatenate([sinq_ref[...]] * MLA_HEADS, axis=1)
    cq = _rms(z[:, A_CQ:A_CQ + Q_LORA_RANK], gq_ref[...]).astype(_BF16)
    q = (_dot(cq, wq_ref[...]) * cosq + _dot(cq, wqrot_ref[...]) * sinq) * (QK_HEAD_DIM ** -0.5)
    q_out[0] = q.astype(_BF16)
    ckv = _rms(z[:, A_CKV:A_CKV + KV_LORA_RANK], gkv_ref[...]).astype(_BF16)
    kpe = (z[:, A_KPE:A_KPE + HEAD_SLAB] * cosq_ref[...]
           + z[:, A_KPE_ROT:A_KPE_ROT + HEAD_SLAB] * sinq_ref[...])
    k = _dot(ckv, wk_ref[...]) + jnp.concatenate([kpe] * MLA_HEADS, axis=1)
    k_out[0] = k.astype(_BF16)
    v_out[0] = _dot_nt(wv_ref[...], ckv).astype(_BF16)

    zr = z[:, A_RWKV:A_RWKV + RWKV_COLS]
    row = lax.broadcasted_iota(jnp.int32, zr.shape, 0)
    prev = jnp.where(row == 0, carry_ref[...], pltpu.roll(zr, shift=1, axis=0))
    carry_ref[...] = zr[tm - 1:tm, :]
    zs = zr + (prev - zr) * mu_ref[...]
    r = zs[:, 0:RWKV_DIM]
    k_raw = zs[:, RWKV_DIM:2 * RWKV_DIM]
    v_r = zs[:, 2 * RWKV_DIM:3 * RWKV_DIM]
    lo = zs[:, 3 * RWKV_DIM:3 * RWKV_DIM + LANES]
    g_lo = zs[:, 3 * RWKV_DIM + LANES:]
    lane = lax.broadcasted_iota(jnp.int32, lo.shape, 1)
    lo = jnp.where(lane < DECAY_LORA, jnp.tanh(lo), lo).astype(_BF16)
    wa = _dot(lo, wwa_ref[...])
    w_pre = w0_ref[...] + wa[:, :RWKV_DIM]
    neg = -w_pre
    softplus = jnp.maximum(neg, 0.0) + jnp.log(1.0 + jnp.exp(-jnp.abs(neg)))
    log_decay = -jnp.exp(-softplus - 0.5)
    a = jax.nn.sigmoid(a0_ref[...] + wa[:, RWKV_DIM:])
    g = _dot(jax.nn.sigmoid(g_lo).astype(_BF16), g2_ref[...])
    kk = k_raw * kk_ref[...]
    k_mod = k_raw * (1.0 + (a - 1.0) * ka_ref[...])
    for hd in range(RWKV_HEADS):
        sl = slice(hd * RWKV_HEAD_DIM, (hd + 1) * RWKV_HEAD_DIM)
        kk_h = kk[:, sl]
        norm = jnp.sqrt(jnp.sum(kk_h * kk_h, axis=-1, keepdims=True))
        kkn_h = kk_h / jnp.maximum(norm, 1e-12)
        r_out[0, hd] = r[:, sl]
        ld_out[0, hd] = log_decay[:, sl]
        kmod_out[0, hd] = k_mod[:, sl]
        vr_out[0, hd] = v_r[:, sl]
        kkn_out[0, hd] = kkn_h
        b_out[0, hd] = kkn_h * a[:, sl]
        g_out[0, hd] = g[:, sl]


def _stage_a(x, cosq, sinq, gmix, win, gq, wq, wqrot, gkv, wk, wv, mu, w0, wwa, a0, g2, kk, ka):
    b, s, _ = x.shape
    tm = min(TM_A, s)
    tok = lambda bi, si: (bi, si, 0)
    head_major = jax.ShapeDtypeStruct((b, RWKV_HEADS, s, RWKV_HEAD_DIM), _F32)
    hm_spec = pl.BlockSpec((1, RWKV_HEADS, tm, RWKV_HEAD_DIM), lambda bi, si: (bi, 0, si, 0))
    weights = (gmix, win, gq, wq, wqrot, gkv, wk, wv, mu, w0, wwa, a0, g2, kk, ka)
    return pl.pallas_call(
        _stage_a_kernel,
        grid=(b, s // tm),
        in_specs=[pl.BlockSpec((1, tm, D_MODEL), tok),
                  pl.BlockSpec((tm, HEAD_SLAB), lambda bi, si: (bi * (s // tm) + si, 0)),
                  pl.BlockSpec((tm, HEAD_SLAB), lambda bi, si: (bi * (s // tm) + si, 0))]
                 + [_resident(w.shape) for w in weights],
        out_specs=[pl.BlockSpec((1, tm, MLA_HEADS * HEAD_SLAB), tok),
                   pl.BlockSpec((1, tm, MLA_HEADS * HEAD_SLAB), tok),
                   pl.BlockSpec((1, MLA_HEADS * V_HEAD_DIM, tm), lambda bi, si: (bi, 0, si))]
                  + [hm_spec] * 7,
        out_shape=[jax.ShapeDtypeStruct((b, s, MLA_HEADS * HEAD_SLAB), _BF16),
                   jax.ShapeDtypeStruct((b, s, MLA_HEADS * HEAD_SLAB), _BF16),
                   jax.ShapeDtypeStruct((b, MLA_HEADS * V_HEAD_DIM, s), _BF16)] + [head_major] * 7,
        scratch_shapes=[pltpu.VMEM((1, RWKV_COLS), _F32)],
        compiler_params=pltpu.CompilerParams(
            dimension_semantics=("arbitrary", "arbitrary"), vmem_limit_bytes=VMEM_LIMIT),
        name="stage_a",
    )(x, cosq, sinq, *weights)


def _attn_kernel(q_ref, k_ref, vt_ref, o_ref):
    tq = q_ref.shape[1]
    nh = q_ref.shape[2] // HEAD_SLAB
    qi = pl.program_id(2)
    key = lax.broadcasted_iota(jnp.int32, (tq, tq), 0)
    qry = lax.broadcasted_iota(jnp.int32, (tq, tq), 1)
    causal = key <= qry
    ones = jnp.ones((ATTN_SUM_ROWS, tq), _BF16)

    def scores(kt, masked):
        off = pl.multiple_of(kt * tq, tq)
        out = []
        for hh in range(nh):
            s = _dot_nt(k_ref[0, pl.ds(off, tq), hh * HEAD_SLAB:(hh + 1) * HEAD_SLAB],
                        q_ref[0, :, hh * HEAD_SLAB:(hh + 1) * HEAD_SLAB])
            if masked:
                s = jnp.where(causal, s, -1e30)
            out.append((s, jnp.max(s, axis=0, keepdims=True)))
        return tuple(out)

    def consume(kt, s_all, stats):
        off = pl.multiple_of(kt * tq, tq)
        probs = []
        for hh in range(nh):
            s, s_max = s_all[hh]
            m_new = jnp.maximum(stats[hh][0], s_max)
            probs.append((m_new, jnp.exp(s - m_new).astype(_BF16)))
        out = []
        for hh in range(nh):
            m, acc = stats[hh]
            m_new, p = probs[hh]
            vt = vt_ref[0, hh * V_HEAD_DIM:(hh + 1) * V_HEAD_DIM, pl.ds(off, tq)]
            acc = jnp.exp(m - m_new) * acc + _dot(jnp.concatenate([vt, ones], axis=0), p)
            out.append((m_new, acc))
        return tuple(out)

    def step(kt, carry):
        pending, s_all, stats = carry
        s_next = scores(kt, masked=False)
        return kt, s_next, consume(pending, s_all, stats)

    init = tuple((jnp.full((1, tq), -1e30, _F32),
                  jnp.zeros((V_HEAD_DIM + ATTN_SUM_ROWS, tq), _F32)) for _ in range(nh))
    pending, s_last, stats = lax.fori_loop(0, qi, step, (qi, scores(qi, masked=True), init))
    stats = consume(pending, s_last, stats)
    outs = [acc[:V_HEAD_DIM] / acc[V_HEAD_DIM:V_HEAD_DIM + 1] for _, acc in stats]
    o_ref[0] = jnp.concatenate(outs, axis=0).T.astype(_BF16)


def _attention(q, k, vt):
    b, s, _ = q.shape
    tq = min(TQ, s)
    groups = MLA_HEADS // ATTN_HEADS
    qk_w = ATTN_HEADS * HEAD_SLAB
    v_w = ATTN_HEADS * V_HEAD_DIM
    return pl.pallas_call(
        _attn_kernel,
        grid=(b, groups, s // tq),
        in_specs=[pl.BlockSpec((1, tq, qk_w), lambda bi, hg, qi: (bi, qi, hg)),
                  pl.BlockSpec((1, s, qk_w), lambda bi, hg, qi: (bi, 0, hg),
                               pipeline_mode=pl.Buffered(1)),
                  pl.BlockSpec((1, v_w, s), lambda bi, hg, qi: (bi, hg, 0),
                               pipeline_mode=pl.Buffered(1))],
        out_specs=pl.BlockSpec((1, tq, v_w), lambda bi, hg, qi: (bi, qi, hg)),
        out_shape=jax.ShapeDtypeStruct((b, s, MLA_HEADS * V_HEAD_DIM), _BF16),
        compiler_params=pltpu.CompilerParams(
            dimension_semantics=("parallel", "parallel", "arbitrary"),
            vmem_limit_bytes=VMEM_LIMIT),
        name="mla_attention",
    )(q, k, vt)


def _bdot(a, b):
    return jnp.einsum("nij,njk->nik", a, b, preferred_element_type=_F32)


def _bdot_nt(a, b):
    return jnp.einsum("nik,njk->nij", a, b, preferred_element_type=_F32)


def _bdot_tn(a, b):
    return jnp.einsum("nki,nkj->nij", a, b, preferred_element_type=_F32)


def _wkv_chunk_terms(r, ld, k, v, kk, b):
    n, c, _ = r.shape
    row = lax.broadcasted_iota(jnp.int32, (c, c), 0)
    col = lax.broadcasted_iota(jnp.int32, (c, c), 1)
    tri = jnp.broadcast_to((col <= row).astype(_BF16)[None], (n, c, c))
    ld_hi = ld.astype(_BF16)
    ld_lo = (ld - ld_hi.astype(_F32)).astype(_BF16)
    cum = _bdot(tri, ld_hi) + _bdot(tri, ld_lo)
    cum_last = cum[:, c - 1:c, :]
    e_neg = jnp.exp(-cum)
    e_tail = jnp.exp(cum_last - cum)
    r_hat = r * jnp.exp(cum)
    a_hat = -kk * jnp.exp(cum - ld)
    a_hat_bf = a_hat.astype(_BF16)
    lhs = jnp.concatenate([a_hat_bf, r_hat.astype(_BF16)], axis=1)
    v_bf = v.astype(_BF16)
    a_b = _bdot_nt(lhs, (b * e_neg).astype(_BF16))
    a_k = _bdot_nt(lhs, (k * e_neg).astype(_BF16))
    strict = (col < row)[None]
    incl = (col <= row)[None]
    a_ab = jnp.where(strict, a_b[:, :c], 0.0)
    a_rb = jnp.where(incl, a_b[:, c:], 0.0).astype(_BF16)
    a_akrk = jnp.concatenate([jnp.where(strict, a_k[:, :c], 0.0),
                              jnp.where(incl, a_k[:, c:], 0.0)], axis=1).astype(_BF16)

    diff = (row ^ col)[None]
    t = jnp.where(diff == 0, 1.0, 0.0) + jnp.where(diff == 1, a_ab, 0.0)
    size = 2
    while size < c:
        a_l = jnp.where((diff >= size) & (diff < 2 * size), a_ab, 0.0).astype(_BF16)
        t_bf = t.astype(_BF16)
        t = t + _bdot(_bdot(t_bf, a_l).astype(_BF16), t_bf)
        size *= 2
    t_bf = t.astype(_BF16)

    v_part = _bdot(a_akrk, v_bf)
    w = _bdot(t_bf, a_hat_bf).astype(_BF16)
    u0 = _bdot(t_bf, v_part[:, :c].astype(_BF16)).astype(_BF16)
    qe = r_hat + _bdot(a_rb, w)
    y0 = v_part[:, c:] + _bdot(a_rb, u0)
    b_bar = (b * e_tail).astype(_BF16)
    g = _bdot_tn(w, b_bar)
    h = _bdot_tn(jnp.concatenate([v_bf, u0], axis=1),
                 jnp.concatenate([(k * e_tail).astype(_BF16), b_bar], axis=1))
    return qe.astype(_BF16), y0, g.astype(_BF16), h, jnp.exp(cum_last)


def _wkv_kernel(r_ref, ld_ref, k_ref, v_ref, kk_ref, b_ref, g_ref, rk_ref, lnw_ref, lnb_ref,
                o_ref, state_ref):
    nh, blk, n = r_ref.shape[1:]
    c = WKV_CHUNK
    nc = blk // c

    @pl.when(pl.program_id(1) == 0)
    def _():
        state_ref[...] = jnp.zeros_like(state_ref)

    chunks = lambda ref: ref[0].reshape(nh * nc, c, n)
    r = r_ref[0]
    k = k_ref[0]
    v = v_ref[0]
    terms = _wkv_chunk_terms(chunks(r_ref), chunks(ld_ref), chunks(k_ref), chunks(v_ref),
                             chunks(kk_ref), chunks(b_ref))
    qe, y0, g, h, decay = (x.reshape((nh, nc) + x.shape[1:]) for x in terms)

    state = state_ref[...]
    ys = []
    for ci in range(nc):
        state_bf = state.astype(_BF16)
        ys.append(_bdot_nt(qe[:, ci], state_bf) + y0[:, ci])
        state = state * decay[:, ci] + _bdot(state_bf, g[:, ci]) + h[:, ci]
    state_ref[...] = state
    y = jnp.concatenate(ys, axis=1)

    mean = jnp.mean(y, axis=-1, keepdims=True)
    var = jnp.mean(jnp.square(y - mean), axis=-1, keepdims=True)
    yn = (y - mean) * lax.rsqrt(var + GN_EPS) * lnw_ref[...][:, None, :] + lnb_ref[...][:, None, :]
    bonus = jnp.sum(r * k * rk_ref[...][:, None, :], axis=-1, keepdims=True) * v
    out = (yn + bonus) * g_ref[0]
    o_ref[0] = jnp.concatenate([out[hd] for hd in range(nh)], axis=-1).astype(_BF16)


def _wkv(r, ld, k, v, kk, bvec, g, rk, lnw, lnb):
    b, nh, s, n = r.shape
    blk = min(WKV_BLOCK, s)
    hm_spec = pl.BlockSpec((1, nh, blk, n), lambda bi, si: (bi, 0, si, 0))
    return pl.pallas_call(
        _wkv_kernel,
        grid=(b, s // blk),
        in_specs=[hm_spec] * 7 + [_resident(rk.shape), _resident(lnw.shape), _resident(lnb.shape)],
        out_specs=pl.BlockSpec((1, blk, nh * n), lambda bi, si: (bi, si, 0)),
        out_shape=jax.ShapeDtypeStruct((b, s, nh * n), _BF16),
        scratch_shapes=[pltpu.VMEM((nh, n, n), _F32)],
        compiler_params=pltpu.CompilerParams(
            dimension_semantics=("parallel", "arbitrary"), vmem_limit_bytes=VMEM_LIMIT),
        name="wkv7",
    )(r, ld, k, v, kk, bvec, g, rk, lnw, lnb)


def _stage_d_kernel(x_ref, oa_ref, ob_ref, p_ref, gmix_ref, wgate_ref, woa_ref, wob_ref, wout_ref,
                    gffn_ref, wup_ref, wdown_ref, gple_ref, wpg_ref, wpp_ref, gfin_ref, out_ref, *,
                    final_norm):
    x = x_ref[...]
    h = _rms(x, gmix_ref[...]).astype(_BF16)
    gate = jax.nn.sigmoid(_dot(h, wgate_ref[...]))
    y_a = _dot(oa_ref[...], woa_ref[...])
    y_b = _dot(ob_ref[...], wob_ref[...])
    mix = gate[:, :D_MODEL] * y_a + gate[:, D_MODEL:] * y_b
    x = x + _dot(mix.astype(_BF16), wout_ref[...])
    h = _rms(x, gffn_ref[...]).astype(_BF16)
    for c0 in range(0, D_FF, FF_CHUNK):
        hid = jnp.square(jnp.maximum(_dot(h, wup_ref[:, c0:c0 + FF_CHUNK]), 0.0))
        x = x + _dot(hid.astype(_BF16), wdown_ref[c0:c0 + FF_CHUNK, :])
    ple_gate = jax.nn.sigmoid(_dot(_rms(x, gple_ref[...]).astype(_BF16), wpg_ref[...]))
    x = x + ple_gate * _dot(p_ref[...].astype(_BF16), wpp_ref[...])
    out_ref[...] = _rms(x, gfin_ref[...]) if final_norm else x


def _stage_d(x, oa, ob, p, gmix, wgate, woa, wob, wout, gffn, wup, wdown, gple, wpg, wpp, gfin,
             final_norm):
    t = x.shape[0]
    tm = min(TM_D, t)
    tok = lambda i: (i, 0)
    weights = (gmix, wgate, woa, wob, wout, gffn, wup, wdown, gple, wpg, wpp, gfin)
    return pl.pallas_call(
        functools.partial(_stage_d_kernel, final_norm=final_norm),
        grid=(t // tm,),
        in_specs=[pl.BlockSpec((tm, D_MODEL), tok),
                  pl.BlockSpec((tm, oa.shape[1]), tok),
                  pl.BlockSpec((tm, ob.shape[1]), tok),
                  pl.BlockSpec((tm, PLE_DIM), tok)] + [_resident(w.shape) for w in weights],
        out_specs=pl.BlockSpec((tm, D_MODEL), tok),
        out_shape=jax.ShapeDtypeStruct((t, D_MODEL), _F32),
        compiler_params=pltpu.CompilerParams(
            dimension_semantics=("parallel",), vmem_limit_bytes=VMEM_LIMIT),
        name="stage_d",
    )(x, oa, ob, p, *weights)


def _head_slabs(w, width, pieces):
    cols = []
    for hd in range(MLA_HEADS):
        used = 0
        for piece in pieces:
            if isinstance(piece, int):
                cols.append(jnp.zeros((w.shape[0], piece), w.dtype))
                used += piece
            else:
                start, stop, sign = piece
                cols.append(sign * w[:, hd * width + start:hd * width + stop])
                used += stop - start
        assert used == HEAD_SLAB
    return jnp.concatenate(cols, axis=1)


def _prepare_layer(i, g_mix, w_in, g_q_a, w_uq, g_kv_a, w_ukv, w_o_mla, mu_rwkv, w0, w2, a0, a2,
                   g2, k_k, k_a, r_k, ln_x_w, ln_x_b, w_o_rwkv, w_out, g_ffn, w_ffn_up, w_ffn_down,
                   g_ple, w_ple_gate, w_ple_proj):
    half = QK_ROPE_DIM // 2
    row = lambda a: a[i].reshape(1, -1)
    w = w_in[i]
    d = w.shape[0]
    kpe = w[:, Q_LORA_RANK + KV_LORA_RANK:MLA_COLS]
    zeros = lambda n: jnp.zeros((d, n), w.dtype)
    kpe_slab = jnp.concatenate([zeros(QK_NOPE_DIM), kpe, zeros(HEAD_SLAB - QK_HEAD_DIM)], axis=1)
    kpe_rot = jnp.concatenate([zeros(QK_NOPE_DIM), -kpe[:, half:], kpe[:, :half],
                               zeros(HEAD_SLAB - QK_HEAD_DIM)], axis=1)
    win = jnp.concatenate([w[:, :Q_LORA_RANK + KV_LORA_RANK], kpe_slab, kpe_rot,
                           w[:, MLA_COLS:MLA_COLS + RWKV_COLS]], axis=1).astype(_BF16)
    wgate = w[:, MLA_COLS + RWKV_COLS:].astype(_BF16)
    pad = HEAD_SLAB - QK_HEAD_DIM
    wq = _head_slabs(w_uq[i], QK_HEAD_DIM, [(0, QK_HEAD_DIM, 1.0), pad]).astype(_BF16)
    wqrot = _head_slabs(w_uq[i], QK_HEAD_DIM,
                        [QK_NOPE_DIM, (QK_NOPE_DIM + half, QK_HEAD_DIM, -1.0),
                         (QK_NOPE_DIM, QK_NOPE_DIM + half, 1.0), pad]).astype(_BF16)
    kv_width = QK_NOPE_DIM + V_HEAD_DIM
    wk = _head_slabs(w_ukv[i], kv_width, [(0, QK_NOPE_DIM, 1.0), HEAD_SLAB - QK_NOPE_DIM]).astype(_BF16)
    wv = jnp.concatenate([w_ukv[i][:, hd * kv_width + QK_NOPE_DIM:(hd + 1) * kv_width]
                          for hd in range(MLA_HEADS)], axis=1).astype(_BF16).T
    wwa = jnp.concatenate(
        [jnp.concatenate([w2[i], jnp.zeros_like(w2[i])], axis=1),
         jnp.concatenate([jnp.zeros_like(a2[i]), a2[i]], axis=1)], axis=0).astype(_BF16)
    stage_a = (row(g_mix), win, row(g_q_a), wq, wqrot, row(g_kv_a), wk, wv, row(mu_rwkv), row(w0),
               wwa, row(a0), g2[i].astype(_BF16), row(k_k), row(k_a))
    wkv = (r_k[i], ln_x_w[i].reshape(RWKV_HEADS, RWKV_HEAD_DIM),
           ln_x_b[i].reshape(RWKV_HEADS, RWKV_HEAD_DIM))
    stage_d = (row(g_mix), wgate, w_o_mla[i].astype(_BF16), w_o_rwkv[i].astype(_BF16),
               w_out[i].astype(_BF16), row(g_ffn), w_ffn_up[i].astype(_BF16),
               w_ffn_down[i].astype(_BF16), row(g_ple), w_ple_gate[i].astype(_BF16),
               w_ple_proj[i].astype(_BF16))
    return stage_a, wkv, stage_d


def kernel(x, p, positions, g_mix, w_in, g_q_a, w_uq, g_kv_a, w_ukv, w_o_mla, mu_rwkv, w0, w2, a0, a2, g2, k_k, k_a, r_k, ln_x_w, ln_x_b, w_o_rwkv, w_out, g_ffn, w_ffn_up, w_ffn_down, g_ple, w_ple_gate, w_ple_proj, g_final):
    b, s, d = x.shape
    depth = w_in.shape[0]
    t = b * s
    cos, sin = _rope_tables(positions)
    ones = jnp.ones((t, QK_NOPE_DIM), _F32)
    zeros = jnp.zeros((t, QK_NOPE_DIM), _F32)
    pad = jnp.zeros((t, HEAD_SLAB - QK_HEAD_DIM), _F32)
    cosq = jnp.concatenate([ones, cos, cos, pad], axis=1)
    sinq = jnp.concatenate([zeros, sin, sin, pad], axis=1)
    gfin = g_final.reshape(1, -1)
    for i in range(depth):
        sa, sw, sd = _prepare_layer(i, g_mix, w_in, g_q_a, w_uq, g_kv_a, w_ukv, w_o_mla, mu_rwkv,
                                    w0, w2, a0, a2, g2, k_k, k_a, r_k, ln_x_w, ln_x_b, w_o_rwkv,
                                    w_out, g_ffn, w_ffn_up, w_ffn_down, g_ple, w_ple_gate,
                                    w_ple_proj)
        q, k, v, r, ld, kmod, vr, kkn, bvec, g = _stage_a(x, cosq, sinq, *sa)
        o_a = _attention(q, k, v)
        o_b = _wkv(r, ld, kmod, vr, kkn, bvec, g, *sw)
        out = _stage_d(x.reshape(t, d), o_a.reshape(t, -1), o_b.reshape(t, -1), p[i].reshape(t, -1),
                       *sd, gfin, final_norm=(i == depth - 1))
        x = out.reshape(b, s, d)
    return x
```

```python
import functools

import jax
import jax.numpy as jnp
from jax import lax
from jax.experimental import pallas as pl
from jax.experimental.pallas import tpu as pltpu

D_MODEL = 1024
MLA_HEADS = 8
QK_NOPE_DIM = 64
QK_ROPE_DIM = 32
QK_HEAD_DIM = QK_NOPE_DIM + QK_ROPE_DIM
V_HEAD_DIM = 64
Q_LORA_RANK = 384
KV_LORA_RANK = 256
ROPE_THETA = 10000.0
RWKV_HEADS = 8
RWKV_HEAD_DIM = 64
RWKV_DIM = RWKV_HEADS * RWKV_HEAD_DIM
DECAY_LORA = 64
AAA_LORA = 64
GATE_LORA = 128
GN_EPS = RWKV_HEAD_DIM * 1e-5
MLA_COLS = Q_LORA_RANK + KV_LORA_RANK + QK_ROPE_DIM
RWKV_COLS = 3 * RWKV_DIM + DECAY_LORA + AAA_LORA + GATE_LORA
D_FF = 4 * D_MODEL
PLE_DIM = 256
RMS_EPS = 1e-6
LOG2_E = 1.4426950408889634

LANES = 128
HEAD_SLAB = LANES
VMEM_LIMIT = 56 * 1024 * 1024

TM_A = 256
TQ = 256
ATTN_HEADS = 8
ATTN_SUM_ROWS = 16
WKV_CHUNK = 64
WKV_BLOCK = 256
TM_D = 256
FF_CHUNK = 1024

A_CQ = 0
A_CKV = A_CQ + Q_LORA_RANK
A_KPE = A_CKV + KV_LORA_RANK
A_KPE_ROT = A_KPE + HEAD_SLAB
A_RWKV = A_KPE_ROT + HEAD_SLAB
A_COLS = A_RWKV + RWKV_COLS

_BF16 = jnp.bfloat16
_F32 = jnp.float32


def _dot(a, b):
    return jnp.dot(a, b, preferred_element_type=_F32)


def _dot_nt(a, b):
    return lax.dot_general(a, b, (((1,), (1,)), ((), ())), preferred_element_type=_F32)


def _dot_tn(a, b):
    return lax.dot_general(a, b, (((0,), (0,)), ((), ())), preferred_element_type=_F32)


def _rms(x, g):
    return x * lax.rsqrt(jnp.mean(x * x, axis=-1, keepdims=True) + RMS_EPS) * g


def _resident(shape):
    zeros = (0,) * len(shape)
    return pl.BlockSpec(shape, lambda *_: zeros, pipeline_mode=pl.Buffered(1))


def _rope_kernel(pos_ref, inv_ref, cos_ref, sin_ref):
    ang = pos_ref[...] * inv_ref[...]
    cos_ref[...] = jnp.cos(ang)
    sin_ref[...] = jnp.sin(ang)


def _rope_tables(positions):
    half = QK_ROPE_DIM // 2
    t = positions.size
    inv_freq = ROPE_THETA ** (-jnp.arange(half, dtype=_F32) / half)
    pos = jnp.repeat(positions.reshape(-1).astype(_F32), half).reshape(t * half // LANES, LANES)
    inv = jnp.tile(inv_freq, LANES // half).reshape(1, LANES)
    cos, sin = pl.pallas_call(
        _rope_kernel,
        out_shape=(jax.ShapeDtypeStruct(pos.shape, _F32),) * 2,
        name="rope_tables",
    )(pos, inv)
    return cos.reshape(t, half), sin.reshape(t, half)


def _stage_a_kernel(x_ref, cosq_ref, sinq_ref, gmix_ref, win_ref, gq_ref, wq_ref, wqrot_ref,
                    gkv_ref, wk_ref, wv_ref, mu_ref, w0_ref, wwa_ref, a0_ref, g2_ref,
                    kk_ref, ka_ref,
                    q_out, k_out, v_out, r_out, ld_out, kmod_out, vr_out, kkn_out, b_out, g_out,
                    carry_ref):
    tm = x_ref.shape[1]

    @pl.when(pl.program_id(1) == 0)
    def _():
        carry_ref[...] = jnp.zeros_like(carry_ref)

    h = _rms(x_ref[0], gmix_ref[...]).astype(_BF16)
    z = _dot(h, win_ref[...])

    cosq = jnp.concatenate([cosq_ref[...]] * MLA_HEADS, axis=1)
    sinq = jnp.concatenate([sinq_ref[...]] * MLA_HEADS, axis=1)
    cq = _rms(z[:, A_CQ:A_CQ + Q_LORA_RANK], gq_ref[...]).astype(_BF16)
    q = (_dot(cq, wq_ref[...]) * cosq + _dot(cq, wqrot_ref[...]) * sinq) * (
        QK_HEAD_DIM ** -0.5 * LOG2_E)
    q_out[0] = q.astype(_BF16)
    ckv = _rms(z[:, A_CKV:A_CKV + KV_LORA_RANK], gkv_ref[...]).astype(_BF16)
    kpe = (z[:, A_KPE:A_KPE + HEAD_SLAB] * cosq_ref[...]
           + z[:, A_KPE_ROT:A_KPE_ROT + HEAD_SLAB] * sinq_ref[...])
    k = _dot(ckv, wk_ref[...]) + jnp.concatenate([kpe] * MLA_HEADS, axis=1)
    k_out[0] = k.astype(_BF16)
    v_out[0] = _dot_nt(wv_ref[...], ckv).astype(_BF16)

    zr = z[:, A_RWKV:A_RWKV + RWKV_COLS]
    row = lax.broadcasted_iota(jnp.int32, zr.shape, 0)
    prev = jnp.where(row == 0, carry_ref[...], pltpu.roll(zr, shift=1, axis=0))
    carry_ref[...] = zr[tm - 1:tm, :]
    zs = zr + (prev - zr) * mu_ref[...]
    r = zs[:, 0:RWKV_DIM]
    k_raw = zs[:, RWKV_DIM:2 * RWKV_DIM]
    v_r = zs[:, 2 * RWKV_DIM:3 * RWKV_DIM]
    lo = zs[:, 3 * RWKV_DIM:3 * RWKV_DIM + LANES]
    g_lo = zs[:, 3 * RWKV_DIM + LANES:]
    lane = lax.broadcasted_iota(jnp.int32, lo.shape, 1)
    lo = jnp.where(lane < DECAY_LORA, jnp.tanh(lo), lo).astype(_BF16)
    wa = _dot(lo, wwa_ref[...])
    w_pre = w0_ref[...] + wa[:, :RWKV_DIM]
    neg = -w_pre
    softplus = jnp.maximum(neg, 0.0) + jnp.log(1.0 + jnp.exp(-jnp.abs(neg)))
    log_decay = -jnp.exp(-softplus - 0.5)
    a = jax.nn.sigmoid(a0_ref[...] + wa[:, RWKV_DIM:])
    g = _dot(jax.nn.sigmoid(g_lo).astype(_BF16), g2_ref[...])
    kk = k_raw * kk_ref[...]
    k_mod = k_raw * (1.0 + (a - 1.0) * ka_ref[...])
    for hd in range(RWKV_HEADS):
        sl = slice(hd * RWKV_HEAD_DIM, (hd + 1) * RWKV_HEAD_DIM)
        kk_h = kk[:, sl]
        norm = jnp.sqrt(jnp.sum(kk_h * kk_h, axis=-1, keepdims=True))
        kkn_h = kk_h / jnp.maximum(norm, 1e-12)
        r_out[0, hd] = r[:, sl]
        ld_out[0, hd] = log_decay[:, sl]
        kmod_out[0, hd] = k_mod[:, sl]
        vr_out[0, hd] = v_r[:, sl]
        kkn_out[0, hd] = kkn_h
        b_out[0, hd] = kkn_h * a[:, sl]
        g_out[0, hd] = g[:, sl]


def _stage_a(x, cosq, sinq, gmix, win, gq, wq, wqrot, gkv, wk, wv, mu, w0, wwa, a0, g2, kk, ka):
    b, s, _ = x.shape
    tm = min(TM_A, s)
    tok = lambda bi, si: (bi, si, 0)
    head_major = jax.ShapeDtypeStruct((b, RWKV_HEADS, s, RWKV_HEAD_DIM), _F32)
    hm_spec = pl.BlockSpec((1, RWKV_HEADS, tm, RWKV_HEAD_DIM), lambda bi, si: (bi, 0, si, 0))
    weights = (gmix, win, gq, wq, wqrot, gkv, wk, wv, mu, w0, wwa, a0, g2, kk, ka)
    return pl.pallas_call(
        _stage_a_kernel,
        grid=(b, s // tm),
        in_specs=[pl.BlockSpec((1, tm, D_MODEL), tok),
                  pl.BlockSpec((tm, HEAD_SLAB), lambda bi, si: (bi * (s // tm) + si, 0)),
                  pl.BlockSpec((tm, HEAD_SLAB), lambda bi, si: (bi * (s // tm) + si, 0))]
                 + [_resident(w.shape) for w in weights],
        out_specs=[pl.BlockSpec((1, tm, MLA_HEADS * HEAD_SLAB), tok),
                   pl.BlockSpec((1, tm, MLA_HEADS * HEAD_SLAB), tok),
                   pl.BlockSpec((1, MLA_HEADS * V_HEAD_DIM, tm), lambda bi, si: (bi, 0, si))]
                  + [hm_spec] * 7,
        out_shape=[jax.ShapeDtypeStruct((b, s, MLA_HEADS * HEAD_SLAB), _BF16),
                   jax.ShapeDtypeStruct((b, s, MLA_HEADS * HEAD_SLAB), _BF16),
                   jax.ShapeDtypeStruct((b, MLA_HEADS * V_HEAD_DIM, s), _BF16)] + [head_major] * 7,
        scratch_shapes=[pltpu.VMEM((1, RWKV_COLS), _F32)],
        compiler_params=pltpu.CompilerParams(
            dimension_semantics=("arbitrary", "arbitrary"), vmem_limit_bytes=VMEM_LIMIT),
        name="stage_a",
    )(x, cosq, sinq, *weights)


def _attn_kernel(q_ref, k_ref, vt_ref, o_ref, s_a, s_b, mx_a, mx_b, m_ref, acc_ref):
    tq = q_ref.shape[1]
    nh = q_ref.shape[2] // HEAD_SLAB
    qi = pl.program_id(2)
    key = lax.broadcasted_iota(jnp.int32, (tq, tq), 0)
    qry = lax.broadcasted_iota(jnp.int32, (tq, tq), 1)
    causal = key <= qry
    ones = jnp.ones((ATTN_SUM_ROWS, tq), _BF16)

    m_ref[...] = jnp.full(m_ref.shape, -1e30, _F32)
    acc_ref[...] = jnp.zeros(acc_ref.shape, _F32)

    def scores(kt, s_buf, mx_buf, masked=False):
        off = pl.multiple_of(kt * tq, tq)
        for hh in range(nh):
            s = _dot_nt(k_ref[0, pl.ds(off, tq), hh * HEAD_SLAB:(hh + 1) * HEAD_SLAB],
                        q_ref[0, :, hh * HEAD_SLAB:(hh + 1) * HEAD_SLAB])
            if masked:
                s = jnp.where(causal, s, -1e30)
            s_buf[hh] = s
            mx_buf[hh] = jnp.max(s, axis=0, keepdims=True)

    def consume(kt, s_buf, mx_buf):
        off = pl.multiple_of(kt * tq, tq)
        for hh in range(nh):
            m = m_ref[hh]
            m_new = jnp.maximum(m, mx_buf[hh])
            p = jnp.exp2(s_buf[hh] - m_new).astype(_BF16)
            vt = vt_ref[0, hh * V_HEAD_DIM:(hh + 1) * V_HEAD_DIM, pl.ds(off, tq)]
            acc_ref[hh] = (jnp.exp2(m - m_new) * acc_ref[hh]
                           + _dot(jnp.concatenate([vt, ones], axis=0), p))
            m_ref[hh] = m_new

    def tile_at(i):
        return jnp.where(i == 0, qi, i - 1)

    scores(qi, s_a, mx_a, masked=True)

    def two_steps(i2, _):
        i = 2 * i2
        scores(i, s_b, mx_b)
        consume(tile_at(i), s_a, mx_a)
        scores(i + 1, s_a, mx_a)
        consume(i, s_b, mx_b)
        return 0

    lax.fori_loop(0, qi // 2, two_steps, 0)

    @pl.when(qi % 2 == 1)
    def _():
        scores(qi - 1, s_b, mx_b)
        consume(tile_at(qi - 1), s_a, mx_a)
        consume(qi - 1, s_b, mx_b)

    @pl.when(qi % 2 == 0)
    def _():
        consume(tile_at(qi), s_a, mx_a)

    outs = [acc_ref[hh, :V_HEAD_DIM] / acc_ref[hh, V_HEAD_DIM:V_HEAD_DIM + 1] for hh in range(nh)]
    o_ref[0] = jnp.concatenate(outs, axis=0).T.astype(_BF16)


def _attention(q, k, vt):
    b, s, _ = q.shape
    tq = min(TQ, s)
    groups = MLA_HEADS // ATTN_HEADS
    qk_w = ATTN_HEADS * HEAD_SLAB
    v_w = ATTN_HEADS * V_HEAD_DIM
    return pl.pallas_call(
        _attn_kernel,
        grid=(b, groups, s // tq),
        in_specs=[pl.BlockSpec((1, tq, qk_w), lambda bi, hg, qi: (bi, qi, hg)),
                  pl.BlockSpec((1, s, qk_w), lambda bi, hg, qi: (bi, 0, hg),
                               pipeline_mode=pl.Buffered(1)),
                  pl.BlockSpec((1, v_w, s), lambda bi, hg, qi: (bi, hg, 0),
                               pipeline_mode=pl.Buffered(1))],
        out_specs=pl.BlockSpec((1, tq, v_w), lambda bi, hg, qi: (bi, qi, hg)),
        out_shape=jax.ShapeDtypeStruct((b, s, MLA_HEADS * V_HEAD_DIM), _BF16),
        scratch_shapes=[pltpu.VMEM((ATTN_HEADS, tq, tq), _F32),
                        pltpu.VMEM((ATTN_HEADS, tq, tq), _F32),
                        pltpu.VMEM((ATTN_HEADS, 1, tq), _F32),
                        pltpu.VMEM((ATTN_HEADS, 1, tq), _F32),
                        pltpu.VMEM((ATTN_HEADS, 1, tq), _F32),
                        pltpu.VMEM((ATTN_HEADS, V_HEAD_DIM + ATTN_SUM_ROWS, tq), _F32)],
        compiler_params=pltpu.CompilerParams(
            dimension_semantics=("parallel", "parallel", "arbitrary"),
            vmem_limit_bytes=VMEM_LIMIT),
        name="mla_attention",
    )(q, k, vt)


def _bdot(a, b):
    return jnp.einsum("nij,njk->nik", a, b, preferred_element_type=_F32)


def _bdot_nt(a, b):
    return jnp.einsum("nik,njk->nij", a, b, preferred_element_type=_F32)


def _bdot_tn(a, b):
    return jnp.einsum("nki,nkj->nij", a, b, preferred_element_type=_F32)


def _wkv_chunk_terms(r, ld, k, v, kk, b):
    n, c, _ = r.shape
    row = lax.broadcasted_iota(jnp.int32, (c, c), 0)
    col = lax.broadcasted_iota(jnp.int32, (c, c), 1)
    tri = jnp.broadcast_to((col <= row).astype(_BF16)[None], (n, c, c))
    ld_hi = ld.astype(_BF16)
    ld_lo = (ld - ld_hi.astype(_F32)).astype(_BF16)
    cum = _bdot(tri, ld_hi) + _bdot(tri, ld_lo)
    cum_last = cum[:, c - 1:c, :]
    e_neg = jnp.exp(-cum)
    e_tail = jnp.exp(cum_last - cum)
    r_hat = r * jnp.exp(cum)
    a_hat = -kk * jnp.exp(cum - ld)
    a_hat_bf = a_hat.astype(_BF16)
    lhs = jnp.concatenate([a_hat_bf, r_hat.astype(_BF16)], axis=1)
    v_bf = v.astype(_BF16)
    a_b = _bdot_nt(lhs, (b * e_neg).astype(_BF16))
    a_k = _bdot_nt(lhs, (k * e_neg).astype(_BF16))
    strict = (col < row)[None]
    incl = (col <= row)[None]
    a_ab = jnp.where(strict, a_b[:, :c], 0.0)
    a_rb = jnp.where(incl, a_b[:, c:], 0.0).astype(_BF16)
    a_akrk = jnp.concatenate([jnp.where(strict, a_k[:, :c], 0.0),
                              jnp.where(incl, a_k[:, c:], 0.0)], axis=1).astype(_BF16)

    diff = (row ^ col)[None]
    t = jnp.where(diff == 0, 1.0, 0.0) + jnp.where(diff == 1, a_ab, 0.0)
    size = 2
    while size < c:
        a_l = jnp.where((diff >= size) & (diff < 2 * size), a_ab, 0.0).astype(_BF16)
        t_bf = t.astype(_BF16)
        t = t + _bdot(_bdot(t_bf, a_l).astype(_BF16), t_bf)
        size *= 2
    t_bf = t.astype(_BF16)

    v_part = _bdot(a_akrk, v_bf)
    w = _bdot(t_bf, a_hat_bf).astype(_BF16)
    u0 = _bdot(t_bf, v_part[:, :c].astype(_BF16)).astype(_BF16)
    qe = r_hat + _bdot(a_rb, w)
    y0 = v_part[:, c:] + _bdot(a_rb, u0)
    b_bar = (b * e_tail).astype(_BF16)
    g = _bdot_tn(w, b_bar)
    h = _bdot_tn(jnp.concatenate([v_bf, u0], axis=1),
                 jnp.concatenate([(k * e_tail).astype(_BF16), b_bar], axis=1))
    return qe.astype(_BF16), y0, g.astype(_BF16), h, jnp.exp(cum_last)


def _wkv_kernel(r_ref, ld_ref, k_ref, v_ref, kk_ref, b_ref, g_ref, rk_ref, lnw_ref, lnb_ref,
                o_ref, state_ref):
    nh, blk, n = r_ref.shape[1:]
    c = WKV_CHUNK
    nc = blk // c

    @pl.when(pl.program_id(1) == 0)
    def _():
        state_ref[...] = jnp.zeros_like(state_ref)

    chunks = lambda ref: ref[0].reshape(nh * nc, c, n)
    r = r_ref[0]
    k = k_ref[0]
    v = v_ref[0]
    terms = _wkv_chunk_terms(chunks(r_ref), chunks(ld_ref), chunks(k_ref), chunks(v_ref),
                             chunks(kk_ref), chunks(b_ref))
    qe, y0, g, h, decay = (x.reshape((nh, nc) + x.shape[1:]) for x in terms)

    state = state_ref[...]
    ys = []
    for ci in range(nc):
        state_bf = state.astype(_BF16)
        ys.append(_bdot_nt(qe[:, ci], state_bf) + y0[:, ci])
        state = state * decay[:, ci] + _bdot(state_bf, g[:, ci]) + h[:, ci]
    state_ref[...] = state
    y = jnp.concatenate(ys, axis=1)

    mean = jnp.mean(y, axis=-1, keepdims=True)
    var = jnp.mean(jnp.square(y - mean), axis=-1, keepdims=True)
    yn = (y - mean) * lax.rsqrt(var + GN_EPS) * lnw_ref[...][:, None, :] + lnb_ref[...][:, None, :]
    bonus = jnp.sum(r * k * rk_ref[...][:, None, :], axis=-1, keepdims=True) * v
    out = (yn + bonus) * g_ref[0]
    o_ref[0] = jnp.concatenate([out[hd] for hd in range(nh)], axis=-1).astype(_BF16)


def _wkv(r, ld, k, v, kk, bvec, g, rk, lnw, lnb):
    b, nh, s, n = r.shape
    blk = min(WKV_BLOCK, s)
    hm_spec = pl.BlockSpec((1, nh, blk, n), lambda bi, si: (bi, 0, si, 0))
    return pl.pallas_call(
        _wkv_kernel,
        grid=(b, s // blk),
        in_specs=[hm_spec] * 7 + [_resident(rk.shape), _resident(lnw.shape), _resident(lnb.shape)],
        out_specs=pl.BlockSpec((1, blk, nh * n), lambda bi, si: (bi, si, 0)),
        out_shape=jax.ShapeDtypeStruct((b, s, nh * n), _BF16),
        scratch_shapes=[pltpu.VMEM((nh, n, n), _F32)],
        compiler_params=pltpu.CompilerParams(
            dimension_semantics=("parallel", "arbitrary"), vmem_limit_bytes=VMEM_LIMIT),
        name="wkv7",
    )(r, ld, k, v, kk, bvec, g, rk, lnw, lnb)


def _stage_d_kernel(x_ref, oa_ref, ob_ref, p_ref, gmix_ref, wgate_ref, woa_ref, wob_ref, wout_ref,
                    gffn_ref, wup_ref, wdown_ref, gple_ref, wpg_ref, wpp_ref, gfin_ref, out_ref, *,
                    final_norm):
    x = x_ref[...]
    h = _rms(x, gmix_ref[...]).astype(_BF16)
    gate = jax.nn.sigmoid(_dot(h, wgate_ref[...]))
    y_a = _dot(oa_ref[...], woa_ref[...])
    y_b = _dot(ob_ref[...], wob_ref[...])
    mix = gate[:, :D_MODEL] * y_a + gate[:, D_MODEL:] * y_b
    x = x + _dot(mix.astype(_BF16), wout_ref[...])
    h = _rms(x, gffn_ref[...]).astype(_BF16)
    for c0 in range(0, D_FF, FF_CHUNK):
        hid = jnp.square(jnp.maximum(_dot(h, wup_ref[:, c0:c0 + FF_CHUNK]), 0.0))
        x = x + _dot(hid.astype(_BF16), wdown_ref[c0:c0 + FF_CHUNK, :])
    ple_gate = jax.nn.sigmoid(_dot(_rms(x, gple_ref[...]).astype(_BF16), wpg_ref[...]))
    x = x + ple_gate * _dot(p_ref[...].astype(_BF16), wpp_ref[...])
    out_ref[...] = _rms(x, gfin_ref[...]) if final_norm else x


def _stage_d(x, oa, ob, p, gmix, wgate, woa, wob, wout, gffn, wup, wdown, gple, wpg, wpp, gfin,
             final_norm):
    t = x.shape[0]
    tm = min(TM_D, t)
    tok = lambda i: (i, 0)
    weights = (gmix, wgate, woa, wob, wout, gffn, wup, wdown, gple, wpg, wpp, gfin)
    return pl.pallas_call(
        functools.partial(_stage_d_kernel, final_norm=final_norm),
        grid=(t // tm,),
        in_specs=[pl.BlockSpec((tm, D_MODEL), tok),
                  pl.BlockSpec((tm, oa.shape[1]), tok),
                  pl.BlockSpec((tm, ob.shape[1]), tok),
                  pl.BlockSpec((tm, PLE_DIM), tok)] + [_resident(w.shape) for w in weights],
        out_specs=pl.BlockSpec((tm, D_MODEL), tok),
        out_shape=jax.ShapeDtypeStruct((t, D_MODEL), _F32),
        compiler_params=pltpu.CompilerParams(
            dimension_semantics=("parallel",), vmem_limit_bytes=VMEM_LIMIT),
        name="stage_d",
    )(x, oa, ob, p, *weights)


def _head_slabs(w, width, pieces):
    cols = []
    for hd in range(MLA_HEADS):
        used = 0
        for piece in pieces:
            if isinstance(piece, int):
                cols.append(jnp.zeros((w.shape[0], piece), w.dtype))
                used += piece
            else:
                start, stop, sign = piece
                cols.append(sign * w[:, hd * width + start:hd * width + stop])
                used += stop - start
        assert used == HEAD_SLAB
    return jnp.concatenate(cols, axis=1)


def _prepare_layer(i, g_mix, w_in, g_q_a, w_uq, g_kv_a, w_ukv, w_o_mla, mu_rwkv, w0, w2, a0, a2,
                   g2, k_k, k_a, r_k, ln_x_w, ln_x_b, w_o_rwkv, w_out, g_ffn, w_ffn_up, w_ffn_down,
                   g_ple, w_ple_gate, w_ple_proj):
    half = QK_ROPE_DIM // 2
    row = lambda a: a[i].reshape(1, -1)
    w = w_in[i]
    d = w.shape[0]
    kpe = w[:, Q_LORA_RANK + KV_LORA_RANK:MLA_COLS]
    zeros = lambda n: jnp.zeros((d, n), w.dtype)
    kpe_slab = jnp.concatenate([zeros(QK_NOPE_DIM), kpe, zeros(HEAD_SLAB - QK_HEAD_DIM)], axis=1)
    kpe_rot = jnp.concatenate([zeros(QK_NOPE_DIM), -kpe[:, half:], kpe[:, :half],
                               zeros(HEAD_SLAB - QK_HEAD_DIM)], axis=1)
    win = jnp.concatenate([w[:, :Q_LORA_RANK + KV_LORA_RANK], kpe_slab, kpe_rot,
                           w[:, MLA_COLS:MLA_COLS + RWKV_COLS]], axis=1).astype(_BF16)
    wgate = w[:, MLA_COLS + RWKV_COLS:].astype(_BF16)
    pad = HEAD_SLAB - QK_HEAD_DIM
    wq = _head_slabs(w_uq[i], QK_HEAD_DIM, [(0, QK_HEAD_DIM, 1.0), pad]).astype(_BF16)
    wqrot = _head_slabs(w_uq[i], QK_HEAD_DIM,
                        [QK_NOPE_DIM, (QK_NOPE_DIM + half, QK_HEAD_DIM, -1.0),
                         (QK_NOPE_DIM, QK_NOPE_DIM + half, 1.0), pad]).astype(_BF16)
    kv_width = QK_NOPE_DIM + V_HEAD_DIM
    wk = _head_slabs(w_ukv[i], kv_width, [(0, QK_NOPE_DIM, 1.0), HEAD_SLAB - QK_NOPE_DIM]).astype(_BF16)
    wv = jnp.concatenate([w_ukv[i][:, hd * kv_width + QK_NOPE_DIM:(hd + 1) * kv_width]
                          for hd in range(MLA_HEADS)], axis=1).astype(_BF16).T
    wwa = jnp.concatenate(
        [jnp.concatenate([w2[i], jnp.zeros_like(w2[i])], axis=1),
         jnp.concatenate([jnp.zeros_like(a2[i]), a2[i]], axis=1)], axis=0).astype(_BF16)
    stage_a = (row(g_mix), win, row(g_q_a), wq, wqrot, row(g_kv_a), wk, wv, row(mu_rwkv), row(w0),
               wwa, row(a0), g2[i].astype(_BF16), row(k_k), row(k_a))
    wkv = (r_k[i], ln_x_w[i].reshape(RWKV_HEADS, RWKV_HEAD_DIM),
           ln_x_b[i].reshape(RWKV_HEADS, RWKV_HEAD_DIM))
    stage_d = (row(g_mix), wgate, w_o_mla[i].astype(_BF16), w_o_rwkv[i].astype(_BF16),
               w_out[i].astype(_BF16), row(g_ffn), w_ffn_up[i].astype(_BF16),
               w_ffn_down[i].astype(_BF16), row(g_ple), w_ple_gate[i].astype(_BF16),
               w_ple_proj[i].astype(_BF16))
    return stage_a, wkv, stage_d


def kernel(x, p, positions, g_mix, w_in, g_q_a, w_uq, g_kv_a, w_ukv, w_o_mla, mu_rwkv, w0, w2, a0, a2, g2, k_k, k_a, r_k, ln_x_w, ln_x_b, w_o_rwkv, w_out, g_ffn, w_ffn_up, w_ffn_down, g_ple, w_ple_gate, w_ple_proj, g_final):
    b, s, d = x.shape
    depth = w_in.shape[0]
    t = b * s
    cos, sin = _rope_tables(positions)
    ones = jnp.ones((t, QK_NOPE_DIM), _F32)
    zeros = jnp.zeros((t, QK_NOPE_DIM), _F32)
    pad = jnp.zeros((t, HEAD_SLAB - QK_HEAD_DIM), _F32)
    cosq = jnp.concatenate([ones, cos, cos, pad], axis=1)
    sinq = jnp.concatenate([zeros, sin, sin, pad], axis=1)
    gfin = g_final.reshape(1, -1)
    for i in range(depth):
        sa, sw, sd = _prepare_layer(i, g_mix, w_in, g_q_a, w_uq, g_kv_a, w_ukv, w_o_mla, mu_rwkv,
                                    w0, w2, a0, a2, g2, k_k, k_a, r_k, ln_x_w, ln_x_b, w_o_rwkv,
                                    w_out, g_ffn, w_ffn_up, w_ffn_down, g_ple, w_ple_gate,
                                    w_ple_proj)
        q, k, v, r, ld, kmod, vr, kkn, bvec, g = _stage_a(x, cosq, sinq, *sa)
        o_a = _attention(q, k, v)
        o_b = _wkv(r, ld, kmod, vr, kkn, bvec, g, *sw)
        out = _stage_d(x.reshape(t, d), o_a.reshape(t, -1), o_b.reshape(t, -1), p[i].reshape(t, -1),
                       *sd, gfin, final_norm=(i == depth - 1))
        x = out.reshape(b, s, d)
    return x
```

```python
import functools

import jax
import jax.numpy as jnp
from jax import lax
from jax.experimental import pallas as pl
from jax.experimental.pallas import tpu as pltpu

D_MODEL = 1024
MLA_HEADS = 8
QK_NOPE_DIM = 64
QK_ROPE_DIM = 32
QK_HEAD_DIM = QK_NOPE_DIM + QK_ROPE_DIM
V_HEAD_DIM = 64
Q_LORA_RANK = 384
KV_LORA_RANK = 256
ROPE_THETA = 10000.0
RWKV_HEADS = 8
RWKV_HEAD_DIM = 64
RWKV_DIM = RWKV_HEADS * RWKV_HEAD_DIM
DECAY_LORA = 64
AAA_LORA = 64
GATE_LORA = 128
GN_EPS = RWKV_HEAD_DIM * 1e-5
MLA_COLS = Q_LORA_RANK + KV_LORA_RANK + QK_ROPE_DIM
RWKV_COLS = 3 * RWKV_DIM + DECAY_LORA + AAA_LORA + GATE_LORA
D_FF = 4 * D_MODEL
PLE_DIM = 256
RMS_EPS = 1e-6
LOG2_E = 1.4426950408889634

LANES = 128
HEAD_SLAB = LANES
VMEM_LIMIT = 56 * 1024 * 1024

TM_A = 512
TQ = 256
ATTN_HEADS = 8
ATTN_SUM_ROWS = 16
WKV_CHUNK = 64
WKV_BLOCK = 512
WKV_GROUP = 256
TM_D = 512
FF_CHUNK = 1024

A_CQ = 0
A_CKV = A_CQ + Q_LORA_RANK
A_KPE = A_CKV + KV_LORA_RANK
A_KPE_ROT = A_KPE + HEAD_SLAB
A_RWKV = A_KPE_ROT + HEAD_SLAB
A_COLS = A_RWKV + RWKV_COLS

_BF16 = jnp.bfloat16
_F32 = jnp.float32


def _dot(a, b):
    return jnp.dot(a, b, preferred_element_type=_F32)


def _dot_nt(a, b):
    return lax.dot_general(a, b, (((1,), (1,)), ((), ())), preferred_element_type=_F32)


def _dot_tn(a, b):
    return lax.dot_general(a, b, (((0,), (0,)), ((), ())), preferred_element_type=_F32)


def _rms(x, g):
    return x * lax.rsqrt(jnp.mean(x * x, axis=-1, keepdims=True) + RMS_EPS) * g


def _same_head(shape):
    row = lax.broadcasted_iota(jnp.int32, shape, len(shape) - 2) // RWKV_HEAD_DIM
    col = lax.broadcasted_iota(jnp.int32, shape, len(shape) - 1) // RWKV_HEAD_DIM
    return row == col


def _head_sums(x):
    ones_bd = jnp.where(_same_head((WKV_GROUP, WKV_GROUP)), 1.0, 0.0).astype(_BF16)
    xb = x.astype(_BF16)
    return jnp.concatenate([_dot(xb[:, g0:g0 + WKV_GROUP], ones_bd)
                            for g0 in range(0, x.shape[1], WKV_GROUP)], axis=1)


def _resident(shape):
    zeros = (0,) * len(shape)
    return pl.BlockSpec(shape, lambda *_: zeros, pipeline_mode=pl.Buffered(1))


def _rope_kernel(pos_ref, inv_ref, cos_ref, sin_ref):
    ang = pos_ref[...] * inv_ref[...]
    cos_ref[...] = jnp.cos(ang)
    sin_ref[...] = jnp.sin(ang)


def _rope_tables(positions):
    half = QK_ROPE_DIM // 2
    t = positions.size
    inv_freq = ROPE_THETA ** (-jnp.arange(half, dtype=_F32) / half)
    pos = jnp.repeat(positions.reshape(-1).astype(_F32), half).reshape(t * half // LANES, LANES)
    inv = jnp.tile(inv_freq, LANES // half).reshape(1, LANES)
    cos, sin = pl.pallas_call(
        _rope_kernel,
        out_shape=(jax.ShapeDtypeStruct(pos.shape, _F32),) * 2,
        name="rope_tables",
    )(pos, inv)
    return cos.reshape(t, half), sin.reshape(t, half)


def _stage_a_kernel(x_ref, cosq_ref, sinq_ref, gmix_ref, win_ref, gq_ref, wq_ref, wqrot_ref,
                    gkv_ref, wk_ref, wv_ref, mu_ref, w0_ref, wwa_ref, a0_ref, g2_ref,
                    kk_ref, ka_ref,
                    q_out, k_out, v_out, r_out, ld_out, kmod_out, vr_out, kkn_out, b_out, g_out,
                    carry_ref):
    tm = x_ref.shape[1]

    @pl.when(pl.program_id(1) == 0)
    def _():
        carry_ref[...] = jnp.zeros_like(carry_ref)

    h = _rms(x_ref[0], gmix_ref[...]).astype(_BF16)
    z = _dot(h, win_ref[...])

    cosq = jnp.concatenate([cosq_ref[...]] * MLA_HEADS, axis=1)
    sinq = jnp.concatenate([sinq_ref[...]] * MLA_HEADS, axis=1)
    cq = _rms(z[:, A_CQ:A_CQ + Q_LORA_RANK], gq_ref[...]).astype(_BF16)
    q = (_dot(cq, wq_ref[...]) * cosq + _dot(cq, wqrot_ref[...]) * sinq) * (
        QK_HEAD_DIM ** -0.5 * LOG2_E)
    q_out[0] = q.astype(_BF16)
    ckv = _rms(z[:, A_CKV:A_CKV + KV_LORA_RANK], gkv_ref[...]).astype(_BF16)
    kpe = (z[:, A_KPE:A_KPE + HEAD_SLAB] * cosq_ref[...]
           + z[:, A_KPE_ROT:A_KPE_ROT + HEAD_SLAB] * sinq_ref[...])
    k = _dot(ckv, wk_ref[...]) + jnp.concatenate([kpe] * MLA_HEADS, axis=1)
    k_out[0] = k.astype(_BF16)
    v_out[0] = _dot_nt(wv_ref[...], ckv).astype(_BF16)

    zr = z[:, A_RWKV:A_RWKV + RWKV_COLS]
    row = lax.broadcasted_iota(jnp.int32, zr.shape, 0)
    prev = jnp.where(row == 0, carry_ref[...], pltpu.roll(zr, shift=1, axis=0))
    carry_ref[...] = zr[tm - 1:tm, :]
    zs = zr + (prev - zr) * mu_ref[...]
    r = zs[:, 0:RWKV_DIM]
    k_raw = zs[:, RWKV_DIM:2 * RWKV_DIM]
    v_r = zs[:, 2 * RWKV_DIM:3 * RWKV_DIM]
    lo = zs[:, 3 * RWKV_DIM:3 * RWKV_DIM + LANES]
    g_lo = zs[:, 3 * RWKV_DIM + LANES:]
    lane = lax.broadcasted_iota(jnp.int32, lo.shape, 1)
    lo = jnp.where(lane < DECAY_LORA, jnp.tanh(lo), lo).astype(_BF16)
    wa = _dot(lo, wwa_ref[...])
    w_pre = w0_ref[...] + wa[:, :RWKV_DIM]
    neg = -w_pre
    softplus = jnp.maximum(neg, 0.0) + jnp.log(1.0 + jnp.exp(-jnp.abs(neg)))
    log_decay = -jnp.exp(-softplus - 0.5)
    a = jax.nn.sigmoid(a0_ref[...] + wa[:, RWKV_DIM:])
    g = _dot(jax.nn.sigmoid(g_lo).astype(_BF16), g2_ref[...])
    kk = k_raw * kk_ref[...]
    kkn = kk / jnp.maximum(jnp.sqrt(_head_sums(kk * kk)), 1e-12)
    r_out[0] = r
    ld_out[0] = log_decay
    kmod_out[0] = k_raw * (1.0 + (a - 1.0) * ka_ref[...])
    vr_out[0] = v_r
    kkn_out[0] = kkn
    b_out[0] = kkn * a
    g_out[0] = g


def _stage_a(x, cosq, sinq, gmix, win, gq, wq, wqrot, gkv, wk, wv, mu, w0, wwa, a0, g2, kk, ka):
    b, s, _ = x.shape
    tm = min(TM_A, s)
    tok = lambda bi, si: (bi, si, 0)
    head_major = jax.ShapeDtypeStruct((b, s, RWKV_DIM), _F32)
    hm_spec = pl.BlockSpec((1, tm, RWKV_DIM), tok)
    weights = (gmix, win, gq, wq, wqrot, gkv, wk, wv, mu, w0, wwa, a0, g2, kk, ka)
    return pl.pallas_call(
        _stage_a_kernel,
        grid=(b, s // tm),
        in_specs=[pl.BlockSpec((1, tm, D_MODEL), tok),
                  pl.BlockSpec((tm, HEAD_SLAB), lambda bi, si: (bi * (s // tm) + si, 0)),
                  pl.BlockSpec((tm, HEAD_SLAB), lambda bi, si: (bi * (s // tm) + si, 0))]
                 + [_resident(w.shape) for w in weights],
        out_specs=[pl.BlockSpec((1, tm, MLA_HEADS * HEAD_SLAB), tok),
                   pl.BlockSpec((1, tm, MLA_HEADS * HEAD_SLAB), tok),
                   pl.BlockSpec((1, MLA_HEADS * V_HEAD_DIM, tm), lambda bi, si: (bi, 0, si))]
                  + [hm_spec] * 7,
        out_shape=[jax.ShapeDtypeStruct((b, s, MLA_HEADS * HEAD_SLAB), _BF16),
                   jax.ShapeDtypeStruct((b, s, MLA_HEADS * HEAD_SLAB), _BF16),
                   jax.ShapeDtypeStruct((b, MLA_HEADS * V_HEAD_DIM, s), _BF16)] + [head_major] * 7,
        scratch_shapes=[pltpu.VMEM((1, RWKV_COLS), _F32)],
        compiler_params=pltpu.CompilerParams(
            dimension_semantics=("arbitrary", "arbitrary"), vmem_limit_bytes=VMEM_LIMIT),
        name="stage_a",
    )(x, cosq, sinq, *weights)


def _attn_kernel(q_ref, k_ref, vt_ref, o_ref, s_a, s_b, mx_a, mx_b, m_ref, acc_ref):
    tq = q_ref.shape[1]
    nh = q_ref.shape[2] // HEAD_SLAB
    qi = pl.program_id(2)
    key = lax.broadcasted_iota(jnp.int32, (tq, tq), 0)
    qry = lax.broadcasted_iota(jnp.int32, (tq, tq), 1)
    causal = key <= qry
    ones = jnp.ones((ATTN_SUM_ROWS, tq), _BF16)

    m_ref[...] = jnp.full(m_ref.shape, -1e30, _F32)
    acc_ref[...] = jnp.zeros(acc_ref.shape, _F32)

    def scores(kt, s_buf, mx_buf, masked=False):
        off = pl.multiple_of(kt * tq, tq)
        for hh in range(nh):
            s = _dot_nt(k_ref[0, pl.ds(off, tq), hh * HEAD_SLAB:(hh + 1) * HEAD_SLAB],
                        q_ref[0, :, hh * HEAD_SLAB:(hh + 1) * HEAD_SLAB])
            if masked:
                s = jnp.where(causal, s, -1e30)
            s_buf[hh] = s
            mx_buf[hh] = jnp.max(s, axis=0, keepdims=True)

    def consume(kt, s_buf, mx_buf):
        off = pl.multiple_of(kt * tq, tq)
        for hh in range(nh):
            m = m_ref[hh]
            m_new = jnp.maximum(m, mx_buf[hh])
            p = jnp.exp2(s_buf[hh] - m_new).astype(_BF16)
            vt = vt_ref[0, hh * V_HEAD_DIM:(hh + 1) * V_HEAD_DIM, pl.ds(off, tq)]
            acc_ref[hh] = (jnp.exp2(m - m_new) * acc_ref[hh]
                           + _dot(jnp.concatenate([vt, ones], axis=0), p))
            m_ref[hh] = m_new

    def tile_at(i):
        return jnp.where(i == 0, qi, i - 1)

    scores(qi, s_a, mx_a, masked=True)

    def two_steps(i2, _):
        i = 2 * i2
        scores(i, s_b, mx_b)
        consume(tile_at(i), s_a, mx_a)
        scores(i + 1, s_a, mx_a)
        consume(i, s_b, mx_b)
        return 0

    lax.fori_loop(0, qi // 2, two_steps, 0)

    @pl.when(qi % 2 == 1)
    def _():
        scores(qi - 1, s_b, mx_b)
        consume(tile_at(qi - 1), s_a, mx_a)
        consume(qi - 1, s_b, mx_b)

    @pl.when(qi % 2 == 0)
    def _():
        consume(tile_at(qi), s_a, mx_a)

    outs = [acc_ref[hh, :V_HEAD_DIM] / acc_ref[hh, V_HEAD_DIM:V_HEAD_DIM + 1] for hh in range(nh)]
    o_ref[0] = jnp.concatenate(outs, axis=0).T.astype(_BF16)


def _attention(q, k, vt):
    b, s, _ = q.shape
    tq = min(TQ, s)
    groups = MLA_HEADS // ATTN_HEADS
    qk_w = ATTN_HEADS * HEAD_SLAB
    v_w = ATTN_HEADS * V_HEAD_DIM
    return pl.pallas_call(
        _attn_kernel,
        grid=(b, groups, s // tq),
        in_specs=[pl.BlockSpec((1, tq, qk_w), lambda bi, hg, qi: (bi, qi, hg)),
                  pl.BlockSpec((1, s, qk_w), lambda bi, hg, qi: (bi, 0, hg),
                               pipeline_mode=pl.Buffered(1)),
                  pl.BlockSpec((1, v_w, s), lambda bi, hg, qi: (bi, hg, 0),
                               pipeline_mode=pl.Buffered(1))],
        out_specs=pl.BlockSpec((1, tq, v_w), lambda bi, hg, qi: (bi, qi, hg)),
        out_shape=jax.ShapeDtypeStruct((b, s, MLA_HEADS * V_HEAD_DIM), _BF16),
        scratch_shapes=[pltpu.VMEM((ATTN_HEADS, tq, tq), _F32),
                        pltpu.VMEM((ATTN_HEADS, tq, tq), _F32),
                        pltpu.VMEM((ATTN_HEADS, 1, tq), _F32),
                        pltpu.VMEM((ATTN_HEADS, 1, tq), _F32),
                        pltpu.VMEM((ATTN_HEADS, 1, tq), _F32),
                        pltpu.VMEM((ATTN_HEADS, V_HEAD_DIM + ATTN_SUM_ROWS, tq), _F32)],
        compiler_params=pltpu.CompilerParams(
            dimension_semantics=("parallel", "parallel", "arbitrary"),
            vmem_limit_bytes=VMEM_LIMIT),
        name="mla_attention",
    )(q, k, vt)


def _bdot(a, b):
    return jnp.einsum("nij,njk->nik", a, b, preferred_element_type=_F32)


def _bdot_nt(a, b):
    return jnp.einsum("nik,njk->nij", a, b, preferred_element_type=_F32)


def _bdot_tn(a, b):
    return jnp.einsum("nki,nkj->nij", a, b, preferred_element_type=_F32)


def _wkv_chunk_terms(r, ld, k, v, kk, b):
    n, c, gw = r.shape
    heads = gw // RWKV_HEAD_DIM
    row = lax.broadcasted_iota(jnp.int32, (c, gw), 0)
    col = lax.broadcasted_iota(jnp.int32, (c, gw), 1) % c
    strict = (col < row)[None]
    incl = (col <= row)[None]
    diff = (row ^ col)[None]
    same_head = _same_head((gw, gw))[None]

    def blockdiag(x):
        return jnp.where(same_head, jnp.concatenate([x] * heads, axis=1), jnp.zeros((), x.dtype))

    trow = lax.broadcasted_iota(jnp.int32, (c, c), 0)
    tcol = lax.broadcasted_iota(jnp.int32, (c, c), 1)
    tri = jnp.broadcast_to((tcol <= trow).astype(_BF16)[None], (n, c, c))
    ld_hi = ld.astype(_BF16)
    ld_lo = (ld - ld_hi.astype(_F32)).astype(_BF16)
    cum = _bdot(tri, ld_hi) + _bdot(tri, ld_lo)
    cum_last = cum[:, c - 1:c, :]
    e_neg = jnp.exp(-cum)
    e_tail = jnp.exp(cum_last - cum)
    r_hat = r * jnp.exp(cum)
    a_hat_bf = (-kk * jnp.exp(cum - ld)).astype(_BF16)
    lhs = jnp.concatenate([a_hat_bf, r_hat.astype(_BF16)], axis=1)
    v_bf = v.astype(_BF16)
    a_b = _bdot_nt(lhs, blockdiag((b * e_neg).astype(_BF16)))
    a_k = _bdot_nt(lhs, blockdiag((k * e_neg).astype(_BF16)))
    a_ab = jnp.where(strict, a_b[:, :c], 0.0)
    a_rb = jnp.where(incl, a_b[:, c:], 0.0).astype(_BF16)
    a_akrk = jnp.concatenate([jnp.where(strict, a_k[:, :c], 0.0),
                              jnp.where(incl, a_k[:, c:], 0.0)], axis=1).astype(_BF16)

    t = jnp.where(diff == 0, 1.0, 0.0) + jnp.where(diff == 1, a_ab, 0.0)
    size = 2
    while size < c:
        a_l = jnp.where((diff >= size) & (diff < 2 * size), a_ab, 0.0).astype(_BF16)
        t_bf = t.astype(_BF16)
        t = t + _bdot(_bdot(t_bf, blockdiag(a_l)).astype(_BF16), blockdiag(t_bf))
        size *= 2
    t_bf = t.astype(_BF16)

    v_part = _bdot(a_akrk, blockdiag(v_bf))
    w = _bdot(t_bf, blockdiag(a_hat_bf)).astype(_BF16)
    u0 = _bdot(t_bf, blockdiag(v_part[:, :c].astype(_BF16))).astype(_BF16)
    qe = r_hat + _bdot(a_rb, blockdiag(w))
    y0 = v_part[:, c:] + _bdot(a_rb, blockdiag(u0))
    b_bar = (b * e_tail).astype(_BF16)
    g = jnp.where(same_head, _bdot_tn(w, b_bar), 0.0)
    h = jnp.where(same_head,
                  _bdot_tn(jnp.concatenate([v_bf, u0], axis=1),
                           jnp.concatenate([(k * e_tail).astype(_BF16), b_bar], axis=1)), 0.0)
    return qe.astype(_BF16), y0, g.astype(_BF16), h, jnp.exp(cum_last)


def _wkv_kernel(r_ref, ld_ref, k_ref, v_ref, kk_ref, b_ref, g_ref, rk_ref, lnw_ref, lnb_ref,
                o_ref, state_ref):
    blk, dim = r_ref.shape[1:]
    c = WKV_CHUNK
    nc = blk // c
    groups = dim // WKV_GROUP

    @pl.when(pl.program_id(1) == 0)
    def _():
        state_ref[...] = jnp.zeros_like(state_ref)

    def problems(ref):
        x = ref[0].reshape(nc, c, dim)
        return jnp.concatenate([x[:, :, g0:g0 + WKV_GROUP] for g0 in range(0, dim, WKV_GROUP)],
                               axis=0)

    terms = _wkv_chunk_terms(problems(r_ref), problems(ld_ref), problems(k_ref), problems(v_ref),
                             problems(kk_ref), problems(b_ref))
    qe, y0, g, h, decay = (x.reshape((groups, nc) + x.shape[1:]) for x in terms)

    state = state_ref[...]
    ys = []
    for ci in range(nc):
        state_bf = state.astype(_BF16)
        ys.append(_bdot_nt(qe[:, ci], state_bf) + y0[:, ci])
        state = state * decay[:, ci] + _bdot(state_bf, g[:, ci]) + h[:, ci]
    state_ref[...] = state
    y = jnp.concatenate(ys, axis=1)
    y = jnp.concatenate([y[gi] for gi in range(groups)], axis=-1)

    inv_n = 1.0 / RWKV_HEAD_DIM
    centered = y - _head_sums(y) * inv_n
    var = _head_sums(centered * centered) * inv_n
    yn = centered * lax.rsqrt(var + GN_EPS) * lnw_ref[...] + lnb_ref[...]
    bonus = _head_sums(r_ref[0] * k_ref[0] * rk_ref[...]) * v_ref[0]
    o_ref[0] = ((yn + bonus) * g_ref[0]).astype(_BF16)


def _wkv(r, ld, k, v, kk, bvec, g, rk, lnw, lnb):
    b, s, dim = r.shape
    blk = min(WKV_BLOCK, s)
    tok_spec = pl.BlockSpec((1, blk, dim), lambda bi, si: (bi, si, 0))
    return pl.pallas_call(
        _wkv_kernel,
        grid=(b, s // blk),
        in_specs=[tok_spec] * 7 + [_resident(rk.shape), _resident(lnw.shape), _resident(lnb.shape)],
        out_specs=tok_spec,
        out_shape=jax.ShapeDtypeStruct((b, s, dim), _BF16),
        scratch_shapes=[pltpu.VMEM((dim // WKV_GROUP, WKV_GROUP, WKV_GROUP), _F32)],
        compiler_params=pltpu.CompilerParams(
            dimension_semantics=("parallel", "arbitrary"), vmem_limit_bytes=VMEM_LIMIT),
        name="wkv7",
    )(r, ld, k, v, kk, bvec, g, rk, lnw, lnb)


def _stage_d_kernel(x_ref, oa_ref, ob_ref, p_ref, gmix_ref, wgate_ref, woa_ref, wob_ref, wout_ref,
                    gffn_ref, wup_ref, wdown_ref, gple_ref, wpg_ref, wpp_ref, gfin_ref, out_ref, *,
                    final_norm):
    x = x_ref[...]
    h = _rms(x, gmix_ref[...]).astype(_BF16)
    gate = jax.nn.sigmoid(_dot(h, wgate_ref[...]))
    y_a = _dot(oa_ref[...], woa_ref[...])
    y_b = _dot(ob_ref[...], wob_ref[...])
    mix = gate[:, :D_MODEL] * y_a + gate[:, D_MODEL:] * y_b
    x = x + _dot(mix.astype(_BF16), wout_ref[...])
    h = _rms(x, gffn_ref[...]).astype(_BF16)
    for c0 in range(0, D_FF, FF_CHUNK):
        hid = jnp.square(jnp.maximum(_dot(h, wup_ref[:, c0:c0 + FF_CHUNK]), 0.0))
        x = x + _dot(hid.astype(_BF16), wdown_ref[c0:c0 + FF_CHUNK, :])
    ple_gate = jax.nn.sigmoid(_dot(_rms(x, gple_ref[...]).astype(_BF16), wpg_ref[...]))
    x = x + ple_gate * _dot(p_ref[...].astype(_BF16), wpp_ref[...])
    out_ref[...] = _rms(x, gfin_ref[...]) if final_norm else x


def _stage_d(x, oa, ob, p, gmix, wgate, woa, wob, wout, gffn, wup, wdown, gple, wpg, wpp, gfin,
             final_norm):
    t = x.shape[0]
    tm = min(TM_D, t)
    tok = lambda i: (i, 0)
    weights = (gmix, wgate, woa, wob, wout, gffn, wup, wdown, gple, wpg, wpp, gfin)
    return pl.pallas_call(
        functools.partial(_stage_d_kernel, final_norm=final_norm),
        grid=(t // tm,),
        in_specs=[pl.BlockSpec((tm, D_MODEL), tok),
                  pl.BlockSpec((tm, oa.shape[1]), tok),
                  pl.BlockSpec((tm, ob.shape[1]), tok),
                  pl.BlockSpec((tm, PLE_DIM), tok)] + [_resident(w.shape) for w in weights],
        out_specs=pl.BlockSpec((tm, D_MODEL), tok),
        out_shape=jax.ShapeDtypeStruct((t, D_MODEL), _F32),
        compiler_params=pltpu.CompilerParams(
            dimension_semantics=("parallel",), vmem_limit_bytes=VMEM_LIMIT),
        name="stage_d",
    )(x, oa, ob, p, *weights)


def _head_slabs(w, width, pieces):
    cols = []
    for hd in range(MLA_HEADS):
        used = 0
        for piece in pieces:
            if isinstance(piece, int):
                cols.append(jnp.zeros((w.shape[0], piece), w.dtype))
                used += piece
            else:
                start, stop, sign = piece
                cols.append(sign * w[:, hd * width + start:hd * width + stop])
                used += stop - start
        assert used == HEAD_SLAB
    return jnp.concatenate(cols, axis=1)


def _prepare_layer(i, g_mix, w_in, g_q_a, w_uq, g_kv_a, w_ukv, w_o_mla, mu_rwkv, w0, w2, a0, a2,
                   g2, k_k, k_a, r_k, ln_x_w, ln_x_b, w_o_rwkv, w_out, g_ffn, w_ffn_up, w_ffn_down,
                   g_ple, w_ple_gate, w_ple_proj):
    half = QK_ROPE_DIM // 2
    row = lambda a: a[i].reshape(1, -1)
    w = w_in[i]
    d = w.shape[0]
    kpe = w[:, Q_LORA_RANK + KV_LORA_RANK:MLA_COLS]
    zeros = lambda n: jnp.zeros((d, n), w.dtype)
    kpe_slab = jnp.concatenate([zeros(QK_NOPE_DIM), kpe, zeros(HEAD_SLAB - QK_HEAD_DIM)], axis=1)
    kpe_rot = jnp.concatenate([zeros(QK_NOPE_DIM), -kpe[:, half:], kpe[:, :half],
                               zeros(HEAD_SLAB - QK_HEAD_DIM)], axis=1)
    win = jnp.concatenate([w[:, :Q_LORA_RANK + KV_LORA_RANK], kpe_slab, kpe_rot,
                           w[:, MLA_COLS:MLA_COLS + RWKV_COLS]], axis=1).astype(_BF16)
    wgate = w[:, MLA_COLS + RWKV_COLS:].astype(_BF16)
    pad = HEAD_SLAB - QK_HEAD_DIM
    wq = _head_slabs(w_uq[i], QK_HEAD_DIM, [(0, QK_HEAD_DIM, 1.0), pad]).astype(_BF16)
    wqrot = _head_slabs(w_uq[i], QK_HEAD_DIM,
                        [QK_NOPE_DIM, (QK_NOPE_DIM + half, QK_HEAD_DIM, -1.0),
                         (QK_NOPE_DIM, QK_NOPE_DIM + half, 1.0), pad]).astype(_BF16)
    kv_width = QK_NOPE_DIM + V_HEAD_DIM
    wk = _head_slabs(w_ukv[i], kv_width, [(0, QK_NOPE_DIM, 1.0), HEAD_SLAB - QK_NOPE_DIM]).astype(_BF16)
    wv = jnp.concatenate([w_ukv[i][:, hd * kv_width + QK_NOPE_DIM:(hd + 1) * kv_width]
                          for hd in range(MLA_HEADS)], axis=1).astype(_BF16).T
    wwa = jnp.concatenate(
        [jnp.concatenate([w2[i], jnp.zeros_like(w2[i])], axis=1),
         jnp.concatenate([jnp.zeros_like(a2[i]), a2[i]], axis=1)], axis=0).astype(_BF16)
    stage_a = (row(g_mix), win, row(g_q_a), wq, wqrot, row(g_kv_a), wk, wv, row(mu_rwkv), row(w0),
               wwa, row(a0), g2[i].astype(_BF16), row(k_k), row(k_a))
    wkv = (r_k[i].reshape(1, -1), row(ln_x_w), row(ln_x_b))
    stage_d = (row(g_mix), wgate, w_o_mla[i].astype(_BF16), w_o_rwkv[i].astype(_BF16),
               w_out[i].astype(_BF16), row(g_ffn), w_ffn_up[i].astype(_BF16),
               w_ffn_down[i].astype(_BF16), row(g_ple), w_ple_gate[i].astype(_BF16),
               w_ple_proj[i].astype(_BF16))
    return stage_a, wkv, stage_d


def kernel(x, p, positions, g_mix, w_in, g_q_a, w_uq, g_kv_a, w_ukv, w_o_mla, mu_rwkv, w0, w2, a0, a2, g2, k_k, k_a, r_k, ln_x_w, ln_x_b, w_o_rwkv, w_out, g_ffn, w_ffn_up, w_ffn_down, g_ple, w_ple_gate, w_ple_proj, g_final):
    b, s, d = x.shape
    depth = w_in.shape[0]
    t = b * s
    cos, sin = _rope_tables(positions)
    ones = jnp.ones((t, QK_NOPE_DIM), _F32)
    zeros = jnp.zeros((t, QK_NOPE_DIM), _F32)
    pad = jnp.zeros((t, HEAD_SLAB - QK_HEAD_DIM), _F32)
    cosq = jnp.concatenate([ones, cos, cos, pad], axis=1)
    sinq = jnp.concatenate([zeros, sin, sin, pad], axis=1)
    gfin = g_final.reshape(1, -1)
    for i in range(depth):
        sa, sw, sd = _prepare_layer(i, g_mix, w_in, g_q_a, w_uq, g_kv_a, w_ukv, w_o_mla, mu_rwkv,
                                    w0, w2, a0, a2, g2, k_k, k_a, r_k, ln_x_w, ln_x_b, w_o_rwkv,
                                    w_out, g_ffn, w_ffn_up, w_ffn_down, g_ple, w_ple_gate,
                                    w_ple_proj)
        q, k, v, r, ld, kmod, vr, kkn, bvec, g = _stage_a(x, cosq, sinq, *sa)
        o_a = _attention(q, k, v)
        o_b = _wkv(r, ld, kmod, vr, kkn, bvec, g, *sw)
        out = _stage_d(x.reshape(t, d), o_a.reshape(t, -1), o_b.reshape(t, -1), p[i].reshape(t, -1),
                       *sd, gfin, final_norm=(i == depth - 1))
        x = out.reshape(b, s, d)
    return x
```

```python
import functools

import jax
import jax.numpy as jnp
from jax import lax
from jax.experimental import pallas as pl
from jax.experimental.pallas import tpu as pltpu

D_MODEL = 1024
MLA_HEADS = 8
QK_NOPE_DIM = 64
QK_ROPE_DIM = 32
QK_HEAD_DIM = QK_NOPE_DIM + QK_ROPE_DIM
V_HEAD_DIM = 64
Q_LORA_RANK = 384
KV_LORA_RANK = 256
ROPE_THETA = 10000.0
RWKV_HEADS = 8
RWKV_HEAD_DIM = 64
RWKV_DIM = RWKV_HEADS * RWKV_HEAD_DIM
DECAY_LORA = 64
AAA_LORA = 64
GATE_LORA = 128
GN_EPS = RWKV_HEAD_DIM * 1e-5
MLA_COLS = Q_LORA_RANK + KV_LORA_RANK + QK_ROPE_DIM
RWKV_COLS = 3 * RWKV_DIM + DECAY_LORA + AAA_LORA + GATE_LORA
D_FF = 4 * D_MODEL
PLE_DIM = 256
RMS_EPS = 1e-6
LOG2_E = 1.4426950408889634

LANES = 128
HEAD_SLAB = LANES
VMEM_LIMIT = 56 * 1024 * 1024

TM_A = 512
TQ = 256
ATTN_HEADS = 8
ATTN_SUM_ROWS = 16
WKV_CHUNK = 64
WKV_BLOCK = 512
WKV_GROUP = 256
TM_D = 512
FF_CHUNK = 1024

A_CQ = 0
A_CKV = A_CQ + Q_LORA_RANK
A_KPE = A_CKV + KV_LORA_RANK
A_KPE_ROT = A_KPE + HEAD_SLAB
A_RWKV = A_KPE_ROT + HEAD_SLAB
A_COLS = A_RWKV + RWKV_COLS

_BF16 = jnp.bfloat16
_F32 = jnp.float32


def _dot(a, b):
    return jnp.dot(a, b, preferred_element_type=_F32)


def _dot_nt(a, b):
    return lax.dot_general(a, b, (((1,), (1,)), ((), ())), preferred_element_type=_F32)


def _dot_tn(a, b):
    return lax.dot_general(a, b, (((0,), (0,)), ((), ())), preferred_element_type=_F32)


def _rms(x, g):
    return x * lax.rsqrt(jnp.mean(x * x, axis=-1, keepdims=True) + RMS_EPS) * g


def _same_head(shape):
    row = lax.broadcasted_iota(jnp.int32, shape, len(shape) - 2) // RWKV_HEAD_DIM
    col = lax.broadcasted_iota(jnp.int32, shape, len(shape) - 1) // RWKV_HEAD_DIM
    return row == col


def _head_sums(x):
    ones_bd = jnp.where(_same_head((WKV_GROUP, WKV_GROUP)), 1.0, 0.0).astype(_BF16)
    xb = x.astype(_BF16)
    return jnp.concatenate([_dot(xb[:, g0:g0 + WKV_GROUP], ones_bd)
                            for g0 in range(0, x.shape[1], WKV_GROUP)], axis=1)


def _resident(shape):
    zeros = (0,) * len(shape)
    return pl.BlockSpec(shape, lambda *_: zeros, pipeline_mode=pl.Buffered(1))


def _rope_kernel(pos_ref, inv_ref, cos_ref, sin_ref):
    ang = pos_ref[...] * inv_ref[...]
    cos_ref[...] = jnp.cos(ang)
    sin_ref[...] = jnp.sin(ang)


def _rope_tables(positions):
    half = QK_ROPE_DIM // 2
    t = positions.size
    inv_freq = ROPE_THETA ** (-jnp.arange(half, dtype=_F32) / half)
    pos = jnp.repeat(positions.reshape(-1).astype(_F32), half).reshape(t * half // LANES, LANES)
    inv = jnp.tile(inv_freq, LANES // half).reshape(1, LANES)
    cos, sin = pl.pallas_call(
        _rope_kernel,
        out_shape=(jax.ShapeDtypeStruct(pos.shape, _F32),) * 2,
        name="rope_tables",
    )(pos, inv)
    return cos.reshape(t, half), sin.reshape(t, half)


def _stage_a_kernel(x_ref, cosq_ref, sinq_ref, cosq_t_ref, sinq_t_ref,
                    gmix_ref, win_ref, gq_ref, wq_ref, wqrot_ref,
                    gkv_ref, wk_ref, wv_ref, mu_ref, w0_ref, wwa_ref, a0_ref, g2_ref,
                    kk_ref, ka_ref,
                    q_out, k_out, v_out, r_out, ld_out, kmod_out, vr_out, kkn_out, b_out, g_out,
                    carry_ref):
    tm = x_ref.shape[1]

    @pl.when(pl.program_id(1) == 0)
    def _():
        carry_ref[...] = jnp.zeros_like(carry_ref)

    h = _rms(x_ref[0], gmix_ref[...]).astype(_BF16)
    z = _dot(h, win_ref[...])

    cosq_t = jnp.concatenate([cosq_t_ref[...]] * MLA_HEADS, axis=0)
    sinq_t = jnp.concatenate([sinq_t_ref[...]] * MLA_HEADS, axis=0)
    cq = _rms(z[:, A_CQ:A_CQ + Q_LORA_RANK], gq_ref[...]).astype(_BF16)
    q_t = (_dot_nt(wq_ref[...], cq) * cosq_t + _dot_nt(wqrot_ref[...], cq) * sinq_t) * (
        QK_HEAD_DIM ** -0.5 * LOG2_E)
    q_out[0] = q_t.astype(_BF16)
    ckv = _rms(z[:, A_CKV:A_CKV + KV_LORA_RANK], gkv_ref[...]).astype(_BF16)
    kpe = (z[:, A_KPE:A_KPE + HEAD_SLAB] * cosq_ref[...]
           + z[:, A_KPE_ROT:A_KPE_ROT + HEAD_SLAB] * sinq_ref[...])
    k = _dot(ckv, wk_ref[...]) + jnp.concatenate([kpe] * MLA_HEADS, axis=1)
    k_out[0] = k.astype(_BF16)
    v_out[0] = _dot_nt(wv_ref[...], ckv).astype(_BF16)

    zr = z[:, A_RWKV:A_RWKV + RWKV_COLS]
    row = lax.broadcasted_iota(jnp.int32, zr.shape, 0)
    prev = jnp.where(row == 0, carry_ref[...], pltpu.roll(zr, shift=1, axis=0))
    carry_ref[...] = zr[tm - 1:tm, :]
    zs = zr + (prev - zr) * mu_ref[...]
    r = zs[:, 0:RWKV_DIM]
    k_raw = zs[:, RWKV_DIM:2 * RWKV_DIM]
    v_r = zs[:, 2 * RWKV_DIM:3 * RWKV_DIM]
    lo = zs[:, 3 * RWKV_DIM:3 * RWKV_DIM + LANES]
    g_lo = zs[:, 3 * RWKV_DIM + LANES:]
    lane = lax.broadcasted_iota(jnp.int32, lo.shape, 1)
    lo = jnp.where(lane < DECAY_LORA, jnp.tanh(lo), lo).astype(_BF16)
    wa = _dot(lo, wwa_ref[...])
    w_pre = w0_ref[...] + wa[:, :RWKV_DIM]
    neg = -w_pre
    softplus = jnp.maximum(neg, 0.0) + jnp.log(1.0 + jnp.exp(-jnp.abs(neg)))
    log_decay = -jnp.exp(-softplus - 0.5)
    a = jax.nn.sigmoid(a0_ref[...] + wa[:, RWKV_DIM:])
    g = _dot(jax.nn.sigmoid(g_lo).astype(_BF16), g2_ref[...])
    kk = k_raw * kk_ref[...]
    kkn = kk / jnp.maximum(jnp.sqrt(_head_sums(kk * kk)), 1e-12)
    r_out[0] = r
    ld_out[0] = log_decay
    kmod_out[0] = k_raw * (1.0 + (a - 1.0) * ka_ref[...])
    vr_out[0] = v_r
    kkn_out[0] = kkn
    b_out[0] = kkn * a
    g_out[0] = g


def _stage_a(x, cosq, sinq, gmix, win, gq, wq, wqrot, gkv, wk, wv, mu, w0, wwa, a0, g2, kk, ka):
    b, s, _ = x.shape
    tm = min(TM_A, s)
    tok = lambda bi, si: (bi, si, 0)
    tok_t = lambda bi, si: (bi, 0, si)
    flat = lambda bi, si: (bi * (s // tm) + si, 0)
    flat_t = lambda bi, si: (0, bi * (s // tm) + si)
    rwkv_tok = jax.ShapeDtypeStruct((b, s, RWKV_DIM), _F32)
    rwkv_spec = pl.BlockSpec((1, tm, RWKV_DIM), tok)
    weights = (gmix, win, gq, wq, wqrot, gkv, wk, wv, mu, w0, wwa, a0, g2, kk, ka)
    return pl.pallas_call(
        _stage_a_kernel,
        grid=(b, s // tm),
        in_specs=[pl.BlockSpec((1, tm, D_MODEL), tok),
                  pl.BlockSpec((tm, HEAD_SLAB), flat),
                  pl.BlockSpec((tm, HEAD_SLAB), flat),
                  pl.BlockSpec((HEAD_SLAB, tm), flat_t),
                  pl.BlockSpec((HEAD_SLAB, tm), flat_t)]
                 + [_resident(w.shape) for w in weights],
        out_specs=[pl.BlockSpec((1, MLA_HEADS * HEAD_SLAB, tm), tok_t),
                   pl.BlockSpec((1, tm, MLA_HEADS * HEAD_SLAB), tok),
                   pl.BlockSpec((1, MLA_HEADS * V_HEAD_DIM, tm), tok_t)]
                  + [rwkv_spec] * 7,
        out_shape=[jax.ShapeDtypeStruct((b, MLA_HEADS * HEAD_SLAB, s), _BF16),
                   jax.ShapeDtypeStruct((b, s, MLA_HEADS * HEAD_SLAB), _BF16),
                   jax.ShapeDtypeStruct((b, MLA_HEADS * V_HEAD_DIM, s), _BF16)] + [rwkv_tok] * 7,
        scratch_shapes=[pltpu.VMEM((1, RWKV_COLS), _F32)],
        compiler_params=pltpu.CompilerParams(
            dimension_semantics=("arbitrary", "arbitrary"), vmem_limit_bytes=VMEM_LIMIT),
        name="stage_a",
    )(x, cosq, sinq, cosq.T, sinq.T, *weights)


def _attn_kernel(q_ref, k_ref, vt_ref, o_ref, s_a, s_b, mx_a, mx_b, m_ref, acc_ref):
    tq = q_ref.shape[2]
    nh = q_ref.shape[1] // HEAD_SLAB
    qi = pl.program_id(2)
    key = lax.broadcasted_iota(jnp.int32, (tq, tq), 0)
    qry = lax.broadcasted_iota(jnp.int32, (tq, tq), 1)
    causal = key <= qry
    ones = jnp.ones((ATTN_SUM_ROWS, tq), _BF16)

    m_ref[...] = jnp.full(m_ref.shape, -1e30, _F32)
    acc_ref[...] = jnp.zeros(acc_ref.shape, _F32)

    def scores(kt, s_buf, mx_buf, masked=False):
        off = pl.multiple_of(kt * tq, tq)
        for hh in range(nh):
            s = _dot(k_ref[0, pl.ds(off, tq), hh * HEAD_SLAB:(hh + 1) * HEAD_SLAB],
                     q_ref[0, hh * HEAD_SLAB:(hh + 1) * HEAD_SLAB, :])
            if masked:
                s = jnp.where(causal, s, -1e30)
            s_buf[hh] = s
            mx_buf[hh] = jnp.max(s, axis=0, keepdims=True)

    def consume(kt, s_buf, mx_buf):
        off = pl.multiple_of(kt * tq, tq)
        for hh in range(nh):
            m = m_ref[hh]
            m_new = jnp.maximum(m, mx_buf[hh])
            p = jnp.exp2(s_buf[hh] - m_new).astype(_BF16)
            vt = vt_ref[0, hh * V_HEAD_DIM:(hh + 1) * V_HEAD_DIM, pl.ds(off, tq)]
            acc_ref[hh] = (jnp.exp2(m - m_new) * acc_ref[hh]
                           + _dot(jnp.concatenate([vt, ones], axis=0), p))
            m_ref[hh] = m_new

    def tile_at(i):
        return jnp.where(i == 0, qi, i - 1)

    scores(qi, s_a, mx_a, masked=True)

    def two_steps(i2, _):
        i = 2 * i2
        scores(i, s_b, mx_b)
        consume(tile_at(i), s_a, mx_a)
        scores(i + 1, s_a, mx_a)
        consume(i, s_b, mx_b)
        return 0

    lax.fori_loop(0, qi // 2, two_steps, 0)

    @pl.when(qi % 2 == 1)
    def _():
        scores(qi - 1, s_b, mx_b)
        consume(tile_at(qi - 1), s_a, mx_a)
        consume(qi - 1, s_b, mx_b)

    @pl.when(qi % 2 == 0)
    def _():
        consume(tile_at(qi), s_a, mx_a)

    outs = [acc_ref[hh, :V_HEAD_DIM] / acc_ref[hh, V_HEAD_DIM:V_HEAD_DIM + 1] for hh in range(nh)]
    o_ref[0] = jnp.concatenate(outs, axis=0).T.astype(_BF16)


def _attention(qt, k, vt):
    b, s, _ = k.shape
    tq = min(TQ, s)
    groups = MLA_HEADS // ATTN_HEADS
    qk_w = ATTN_HEADS * HEAD_SLAB
    v_w = ATTN_HEADS * V_HEAD_DIM
    return pl.pallas_call(
        _attn_kernel,
        grid=(b, groups, s // tq),
        in_specs=[pl.BlockSpec((1, qk_w, tq), lambda bi, hg, qi: (bi, hg, qi)),
                  pl.BlockSpec((1, s, qk_w), lambda bi, hg, qi: (bi, 0, hg),
                               pipeline_mode=pl.Buffered(1)),
                  pl.BlockSpec((1, v_w, s), lambda bi, hg, qi: (bi, hg, 0),
                               pipeline_mode=pl.Buffered(1))],
        out_specs=pl.BlockSpec((1, tq, v_w), lambda bi, hg, qi: (bi, qi, hg)),
        out_shape=jax.ShapeDtypeStruct((b, s, MLA_HEADS * V_HEAD_DIM), _BF16),
        scratch_shapes=[pltpu.VMEM((ATTN_HEADS, tq, tq), _F32),
                        pltpu.VMEM((ATTN_HEADS, tq, tq), _F32),
                        pltpu.VMEM((ATTN_HEADS, 1, tq), _F32),
                        pltpu.VMEM((ATTN_HEADS, 1, tq), _F32),
                        pltpu.VMEM((ATTN_HEADS, 1, tq), _F32),
                        pltpu.VMEM((ATTN_HEADS, V_HEAD_DIM + ATTN_SUM_ROWS, tq), _F32)],
        compiler_params=pltpu.CompilerParams(
            dimension_semantics=("parallel", "parallel", "arbitrary"),
            vmem_limit_bytes=VMEM_LIMIT),
        name="mla_attention",
    )(qt, k, vt)


def _bdot(a, b):
    return jnp.einsum("nij,njk->nik", a, b, preferred_element_type=_F32)


def _bdot_nt(a, b):
    return jnp.einsum("nik,njk->nij", a, b, preferred_element_type=_F32)


def _bdot_tn(a, b):
    return jnp.einsum("nki,nkj->nij", a, b, preferred_element_type=_F32)


def _wkv_chunk_terms(r, ld, k, v, kk, b):
    n, c, gw = r.shape
    heads = gw // RWKV_HEAD_DIM
    row = lax.broadcasted_iota(jnp.int32, (c, gw), 0)
    col = lax.broadcasted_iota(jnp.int32, (c, gw), 1) % c
    strict = (col < row)[None]
    incl = (col <= row)[None]
    diff = (row ^ col)[None]
    same_head = _same_head((gw, gw))[None]

    def blockdiag(x):
        return jnp.where(same_head, jnp.concatenate([x] * heads, axis=1), jnp.zeros((), x.dtype))

    trow = lax.broadcasted_iota(jnp.int32, (c, c), 0)
    tcol = lax.broadcasted_iota(jnp.int32, (c, c), 1)
    tri = jnp.broadcast_to((tcol <= trow).astype(_BF16)[None], (n, c, c))
    ld_hi = ld.astype(_BF16)
    ld_lo = (ld - ld_hi.astype(_F32)).astype(_BF16)
    cum = _bdot(tri, ld_hi) + _bdot(tri, ld_lo)
    cum_last = cum[:, c - 1:c, :]
    e_neg = jnp.exp(-cum)
    e_tail = jnp.exp(cum_last - cum)
    r_hat = r * jnp.exp(cum)
    a_hat_bf = (-kk * jnp.exp(cum - ld)).astype(_BF16)
    lhs = jnp.concatenate([a_hat_bf, r_hat.astype(_BF16)], axis=1)
    v_bf = v.astype(_BF16)
    a_b = _bdot_nt(lhs, blockdiag((b * e_neg).astype(_BF16)))
    a_k = _bdot_nt(lhs, blockdiag((k * e_neg).astype(_BF16)))
    a_ab = jnp.where(strict, a_b[:, :c], 0.0)
    a_rb = jnp.where(incl, a_b[:, c:], 0.0).astype(_BF16)
    a_akrk = jnp.concatenate([jnp.where(strict, a_k[:, :c], 0.0),
                              jnp.where(incl, a_k[:, c:], 0.0)], axis=1).astype(_BF16)

    t = jnp.where(diff == 0, 1.0, 0.0) + jnp.where(diff == 1, a_ab, 0.0)
    size = 2
    while size < c:
        a_l = jnp.where((diff >= size) & (diff < 2 * size), a_ab, 0.0).astype(_BF16)
        t_bf = t.astype(_BF16)
        t = t + _bdot(_bdot(t_bf, blockdiag(a_l)).astype(_BF16), blockdiag(t_bf))
        size *= 2
    t_bf = t.astype(_BF16)

    v_part = _bdot(a_akrk, blockdiag(v_bf))
    w = _bdot(t_bf, blockdiag(a_hat_bf)).astype(_BF16)
    u0 = _bdot(t_bf, blockdiag(v_part[:, :c].astype(_BF16))).astype(_BF16)
    qe = r_hat + _bdot(a_rb, blockdiag(w))
    y0 = v_part[:, c:] + _bdot(a_rb, blockdiag(u0))
    b_bar = (b * e_tail).astype(_BF16)
    g = jnp.where(same_head, _bdot_tn(w, b_bar), 0.0)
    h = jnp.where(same_head,
                  _bdot_tn(jnp.concatenate([v_bf, u0], axis=1),
                           jnp.concatenate([(k * e_tail).astype(_BF16), b_bar], axis=1)), 0.0)
    return qe.astype(_BF16), y0, g.astype(_BF16), h, jnp.exp(cum_last)


def _wkv_kernel(r_ref, ld_ref, k_ref, v_ref, kk_ref, b_ref, g_ref, rk_ref, lnw_ref, lnb_ref,
                o_ref, state_ref):
    blk, dim = r_ref.shape[1:]
    c = WKV_CHUNK
    nc = blk // c
    groups = dim // WKV_GROUP

    @pl.when(pl.program_id(1) == 0)
    def _():
        state_ref[...] = jnp.zeros_like(state_ref)

    def problems(ref):
        x = ref[0].reshape(nc, c, dim)
        return jnp.concatenate([x[:, :, g0:g0 + WKV_GROUP] for g0 in range(0, dim, WKV_GROUP)],
                               axis=0)

    terms = _wkv_chunk_terms(problems(r_ref), problems(ld_ref), problems(k_ref), problems(v_ref),
                             problems(kk_ref), problems(b_ref))
    qe, y0, g, h, decay = (x.reshape((groups, nc) + x.shape[1:]) for x in terms)

    state = state_ref[...]
    ys = []
    for ci in range(nc):
        state_bf = state.astype(_BF16)
        ys.append(_bdot_nt(qe[:, ci], state_bf) + y0[:, ci])
        state = state * decay[:, ci] + _bdot(state_bf, g[:, ci]) + h[:, ci]
    state_ref[...] = state
    y = jnp.concatenate(ys, axis=1)
    y = jnp.concatenate([y[gi] for gi in range(groups)], axis=-1)

    inv_n = 1.0 / RWKV_HEAD_DIM
    centered = y - _head_sums(y) * inv_n
    var = _head_sums(centered * centered) * inv_n
    yn = centered * lax.rsqrt(var + GN_EPS) * lnw_ref[...] + lnb_ref[...]
    bonus = _head_sums(r_ref[0] * k_ref[0] * rk_ref[...]) * v_ref[0]
    o_ref[0] = ((yn + bonus) * g_ref[0]).astype(_BF16)


def _wkv(r, ld, k, v, kk, bvec, g, rk, lnw, lnb):
    b, s, dim = r.shape
    blk = min(WKV_BLOCK, s)
    tok_spec = pl.BlockSpec((1, blk, dim), lambda bi, si: (bi, si, 0))
    return pl.pallas_call(
        _wkv_kernel,
        grid=(b, s // blk),
        in_specs=[tok_spec] * 7 + [_resident(rk.shape), _resident(lnw.shape), _resident(lnb.shape)],
        out_specs=tok_spec,
        out_shape=jax.ShapeDtypeStruct((b, s, dim), _BF16),
        scratch_shapes=[pltpu.VMEM((dim // WKV_GROUP, WKV_GROUP, WKV_GROUP), _F32)],
        compiler_params=pltpu.CompilerParams(
            dimension_semantics=("parallel", "arbitrary"), vmem_limit_bytes=VMEM_LIMIT),
        name="wkv7",
    )(r, ld, k, v, kk, bvec, g, rk, lnw, lnb)


def _stage_d_kernel(x_ref, oa_ref, ob_ref, p_ref, gmix_ref, wgate_ref, woa_ref, wob_ref, wout_ref,
                    gffn_ref, wup_ref, wdown_ref, gple_ref, wpg_ref, wpp_ref, gfin_ref, out_ref, *,
                    final_norm):
    x = x_ref[...]
    h = _rms(x, gmix_ref[...]).astype(_BF16)
    gate = jax.nn.sigmoid(_dot(h, wgate_ref[...]))
    y_a = _dot(oa_ref[...], woa_ref[...])
    y_b = _dot(ob_ref[...], wob_ref[...])
    mix = gate[:, :D_MODEL] * y_a + gate[:, D_MODEL:] * y_b
    x = x + _dot(mix.astype(_BF16), wout_ref[...])
    h = _rms(x, gffn_ref[...]).astype(_BF16)
    for c0 in range(0, D_FF, FF_CHUNK):
        hid = jnp.square(jnp.maximum(_dot(h, wup_ref[:, c0:c0 + FF_CHUNK]), 0.0))
        x = x + _dot(hid.astype(_BF16), wdown_ref[c0:c0 + FF_CHUNK, :])
    ple_gate = jax.nn.sigmoid(_dot(_rms(x, gple_ref[...]).astype(_BF16), wpg_ref[...]))
    x = x + ple_gate * _dot(p_ref[...].astype(_BF16), wpp_ref[...])
    out_ref[...] = _rms(x, gfin_ref[...]) if final_norm else x


def _stage_d(x, oa, ob, p, gmix, wgate, woa, wob, wout, gffn, wup, wdown, gple, wpg, wpp, gfin,
             final_norm):
    t = x.shape[0]
    tm = min(TM_D, t)
    tok = lambda i: (i, 0)
    weights = (gmix, wgate, woa, wob, wout, gffn, wup, wdown, gple, wpg, wpp, gfin)
    return pl.pallas_call(
        functools.partial(_stage_d_kernel, final_norm=final_norm),
        grid=(t // tm,),
        in_specs=[pl.BlockSpec((tm, D_MODEL), tok),
                  pl.BlockSpec((tm, oa.shape[1]), tok),
                  pl.BlockSpec((tm, ob.shape[1]), tok),
                  pl.BlockSpec((tm, PLE_DIM), tok)] + [_resident(w.shape) for w in weights],
        out_specs=pl.BlockSpec((tm, D_MODEL), tok),
        out_shape=jax.ShapeDtypeStruct((t, D_MODEL), _F32),
        compiler_params=pltpu.CompilerParams(
            dimension_semantics=("parallel",), vmem_limit_bytes=VMEM_LIMIT),
        name="stage_d",
    )(x, oa, ob, p, *weights)


def _head_slabs(w, width, pieces):
    cols = []
    for hd in range(MLA_HEADS):
        used = 0
        for piece in pieces:
            if isinstance(piece, int):
                cols.append(jnp.zeros((w.shape[0], piece), w.dtype))
                used += piece
            else:
                start, stop, sign = piece
                cols.append(sign * w[:, hd * width + start:hd * width + stop])
                used += stop - start
        assert used == HEAD_SLAB
    return jnp.concatenate(cols, axis=1)


def _prepare_layer(i, g_mix, w_in, g_q_a, w_uq, g_kv_a, w_ukv, w_o_mla, mu_rwkv, w0, w2, a0, a2,
                   g2, k_k, k_a, r_k, ln_x_w, ln_x_b, w_o_rwkv, w_out, g_ffn, w_ffn_up, w_ffn_down,
                   g_ple, w_ple_gate, w_ple_proj):
    half = QK_ROPE_DIM // 2
    row = lambda a: a[i].reshape(1, -1)
    w = w_in[i]
    d = w.shape[0]
    kpe = w[:, Q_LORA_RANK + KV_LORA_RANK:MLA_COLS]
    zeros = lambda n: jnp.zeros((d, n), w.dtype)
    kpe_slab = jnp.concatenate([zeros(QK_NOPE_DIM), kpe, zeros(HEAD_SLAB - QK_HEAD_DIM)], axis=1)
    kpe_rot = jnp.concatenate([zeros(QK_NOPE_DIM), -kpe[:, half:], kpe[:, :half],
                               zeros(HEAD_SLAB - QK_HEAD_DIM)], axis=1)
    win = jnp.concatenate([w[:, :Q_LORA_RANK + KV_LORA_RANK], kpe_slab, kpe_rot,
                           w[:, MLA_COLS:MLA_COLS + RWKV_COLS]], axis=1).astype(_BF16)
    wgate = w[:, MLA_COLS + RWKV_COLS:].astype(_BF16)
    pad = HEAD_SLAB - QK_HEAD_DIM
    wq = _head_slabs(w_uq[i], QK_HEAD_DIM, [(0, QK_HEAD_DIM, 1.0), pad]).astype(_BF16).T
    wqrot = _head_slabs(w_uq[i], QK_HEAD_DIM,
                        [QK_NOPE_DIM, (QK_NOPE_DIM + half, QK_HEAD_DIM, -1.0),
                         (QK_NOPE_DIM, QK_NOPE_DIM + half, 1.0), pad]).astype(_BF16).T
    kv_width = QK_NOPE_DIM + V_HEAD_DIM
    wk = _head_slabs(w_ukv[i], kv_width, [(0, QK_NOPE_DIM, 1.0), HEAD_SLAB - QK_NOPE_DIM]).astype(_BF16)
    wv = jnp.concatenate([w_ukv[i][:, hd * kv_width + QK_NOPE_DIM:(hd + 1) * kv_width]
                          for hd in range(MLA_HEADS)], axis=1).astype(_BF16).T
    wwa = jnp.concatenate(
        [jnp.concatenate([w2[i], jnp.zeros_like(w2[i])], axis=1),
         jnp.concatenate([jnp.zeros_like(a2[i]), a2[i]], axis=1)], axis=0).astype(_BF16)
    stage_a = (row(g_mix), win, row(g_q_a), wq, wqrot, row(g_kv_a), wk, wv, row(mu_rwkv), row(w0),
               wwa, row(a0), g2[i].astype(_BF16), row(k_k), row(k_a))
    wkv = (r_k[i].reshape(1, -1), row(ln_x_w), row(ln_x_b))
    stage_d = (row(g_mix), wgate, w_o_mla[i].astype(_BF16), w_o_rwkv[i].astype(_BF16),
               w_out[i].astype(_BF16), row(g_ffn), w_ffn_up[i].astype(_BF16),
               w_ffn_down[i].astype(_BF16), row(g_ple), w_ple_gate[i].astype(_BF16),
               w_ple_proj[i].astype(_BF16))
    return stage_a, wkv, stage_d


def kernel(x, p, positions, g_mix, w_in, g_q_a, w_uq, g_kv_a, w_ukv, w_o_mla, mu_rwkv, w0, w2, a0, a2, g2, k_k, k_a, r_k, ln_x_w, ln_x_b, w_o_rwkv, w_out, g_ffn, w_ffn_up, w_ffn_down, g_ple, w_ple_gate, w_ple_proj, g_final):
    b, s, d = x.shape
    depth = w_in.shape[0]
    t = b * s
    cos, sin = _rope_tables(positions)
    ones = jnp.ones((t, QK_NOPE_DIM), _F32)
    zeros = jnp.zeros((t, QK_NOPE_DIM), _F32)
    pad = jnp.zeros((t, HEAD_SLAB - QK_HEAD_DIM), _F32)
    cosq = jnp.concatenate([ones, cos, cos, pad], axis=1)
    sinq = jnp.concatenate([zeros, sin, sin, pad], axis=1)
    gfin = g_final.reshape(1, -1)
    for i in range(depth):
        sa, sw, sd = _prepare_layer(i, g_mix, w_in, g_q_a, w_uq, g_kv_a, w_ukv, w_o_mla, mu_rwkv,
                                    w0, w2, a0, a2, g2, k_k, k_a, r_k, ln_x_w, ln_x_b, w_o_rwkv,
                                    w_out, g_ffn, w_ffn_up, w_ffn_down, g_ple, w_ple_gate,
                                    w_ple_proj)
        q, k, v, r, ld, kmod, vr, kkn, bvec, g = _stage_a(x, cosq, sinq, *sa)
        o_a = _attention(q, k, v)
        o_b = _wkv(r, ld, kmod, vr, kkn, bvec, g, *sw)
        out = _stage_d(x.reshape(t, d), o_a.reshape(t, -1), o_b.reshape(t, -1), p[i].reshape(t, -1),
                       *sd, gfin, final_norm=(i == depth - 1))
        x = out.reshape(b, s, d)
    return x
```

```python
import functools

import jax
import jax.numpy as jnp
from jax import lax
from jax.experimental import pallas as pl
from jax.experimental.pallas import tpu as pltpu

D_MODEL = 1024
MLA_HEADS = 8
QK_NOPE_DIM = 64
QK_ROPE_DIM = 32
QK_HEAD_DIM = QK_NOPE_DIM + QK_ROPE_DIM
V_HEAD_DIM = 64
Q_LORA_RANK = 384
KV_LORA_RANK = 256
ROPE_THETA = 10000.0
RWKV_HEADS = 8
RWKV_HEAD_DIM = 64
RWKV_DIM = RWKV_HEADS * RWKV_HEAD_DIM
DECAY_LORA = 64
AAA_LORA = 64
GATE_LORA = 128
GN_EPS = RWKV_HEAD_DIM * 1e-5
MLA_COLS = Q_LORA_RANK + KV_LORA_RANK + QK_ROPE_DIM
RWKV_COLS = 3 * RWKV_DIM + DECAY_LORA + AAA_LORA + GATE_LORA
D_FF = 4 * D_MODEL
PLE_DIM = 256
RMS_EPS = 1e-6
LOG2_E = 1.4426950408889634

LANES = 128
HEAD_SLAB = LANES
VMEM_LIMIT = 56 * 1024 * 1024

TM_A = 512
TQ = 256
ATTN_HEADS = 8
ATTN_SUM_ROWS = 16
WKV_CHUNK = 64
WKV_BLOCK = 512
WKV_GROUP = 256
TM_D = 512
FF_CHUNK = 1024

A_CQ = 0
A_CKV = A_CQ + Q_LORA_RANK
A_KPE = A_CKV + KV_LORA_RANK
A_KPE_ROT = A_KPE + HEAD_SLAB
A_RWKV = A_KPE_ROT + HEAD_SLAB
A_COLS = A_RWKV + RWKV_COLS

_BF16 = jnp.bfloat16
_F32 = jnp.float32


def _dot(a, b):
    return jnp.dot(a, b, preferred_element_type=_F32)


def _dot_nt(a, b):
    return lax.dot_general(a, b, (((1,), (1,)), ((), ())), preferred_element_type=_F32)


def _dot_tn(a, b):
    return lax.dot_general(a, b, (((0,), (0,)), ((), ())), preferred_element_type=_F32)


def _rms(x, g):
    return x * lax.rsqrt(jnp.mean(x * x, axis=-1, keepdims=True) + RMS_EPS) * g


def _same_head(shape):
    row = lax.broadcasted_iota(jnp.int32, shape, len(shape) - 2) // RWKV_HEAD_DIM
    col = lax.broadcasted_iota(jnp.int32, shape, len(shape) - 1) // RWKV_HEAD_DIM
    return row == col


def _head_sums(x):
    ones_bd = jnp.where(_same_head((WKV_GROUP, WKV_GROUP)), 1.0, 0.0).astype(_BF16)
    xb = x.astype(_BF16)
    return jnp.concatenate([_dot(xb[:, g0:g0 + WKV_GROUP], ones_bd)
                            for g0 in range(0, x.shape[1], WKV_GROUP)], axis=1)


def _resident(shape):
    zeros = (0,) * len(shape)
    return pl.BlockSpec(shape, lambda *_: zeros, pipeline_mode=pl.Buffered(1))


def _rope_kernel(pos_ref, inv_ref, cos_ref, sin_ref):
    ang = pos_ref[...] * inv_ref[...]
    cos_ref[...] = jnp.cos(ang)
    sin_ref[...] = jnp.sin(ang)


def _rope_tables(positions):
    half = QK_ROPE_DIM // 2
    t = positions.size
    inv_freq = ROPE_THETA ** (-jnp.arange(half, dtype=_F32) / half)
    return pl.pallas_call(
        _rope_kernel,
        out_shape=(jax.ShapeDtypeStruct((half, t), _F32),) * 2,
        name="rope_tables",
    )(positions.reshape(1, t).astype(_F32), inv_freq.reshape(half, 1))


def _stage_a_kernel(x_ref, cosq_ref, sinq_ref, cos_t_ref, sin_t_ref,
                    gmix_ref, win_ref, gq_ref, wq_ref, wqrot_ref,
                    gkv_ref, wk_ref, wv_ref, mu_ref, w0_ref, wwa_ref, a0_ref, g2_ref,
                    kk_ref, ka_ref,
                    q_out, k_out, v_out, r_out, ld_out, kmod_out, vr_out, kkn_out, b_out, g_out,
                    carry_ref):
    tm = x_ref.shape[1]

    @pl.when(pl.program_id(1) == 0)
    def _():
        carry_ref[...] = jnp.zeros_like(carry_ref)

    h = _rms(x_ref[0], gmix_ref[...]).astype(_BF16)
    z = _dot(h, win_ref[...])

    cos_t = cos_t_ref[...]
    sin_t = sin_t_ref[...]
    nope = (QK_NOPE_DIM, tm)
    pad = jnp.zeros((HEAD_SLAB - QK_HEAD_DIM, tm), _F32)
    cosq_t = jnp.concatenate([jnp.ones(nope, _F32), cos_t, cos_t, pad] * MLA_HEADS, axis=0)
    sinq_t = jnp.concatenate([jnp.zeros(nope, _F32), sin_t, sin_t, pad] * MLA_HEADS, axis=0)
    cq = _rms(z[:, A_CQ:A_CQ + Q_LORA_RANK], gq_ref[...]).astype(_BF16)
    q_t = (_dot_nt(wq_ref[...], cq) * cosq_t + _dot_nt(wqrot_ref[...], cq) * sinq_t) * (
        QK_HEAD_DIM ** -0.5 * LOG2_E)
    q_out[0] = q_t.astype(_BF16)
    ckv = _rms(z[:, A_CKV:A_CKV + KV_LORA_RANK], gkv_ref[...]).astype(_BF16)
    kpe = (z[:, A_KPE:A_KPE + HEAD_SLAB] * cosq_ref[...]
           + z[:, A_KPE_ROT:A_KPE_ROT + HEAD_SLAB] * sinq_ref[...])
    k = _dot(ckv, wk_ref[...]) + jnp.concatenate([kpe] * MLA_HEADS, axis=1)
    k_out[0] = k.astype(_BF16)
    v_out[0] = _dot_nt(wv_ref[...], ckv).astype(_BF16)

    zr = z[:, A_RWKV:A_RWKV + RWKV_COLS]
    row = lax.broadcasted_iota(jnp.int32, zr.shape, 0)
    prev = jnp.where(row == 0, carry_ref[...], pltpu.roll(zr, shift=1, axis=0))
    carry_ref[...] = zr[tm - 1:tm, :]
    zs = zr + (prev - zr) * mu_ref[...]
    r = zs[:, 0:RWKV_DIM]
    k_raw = zs[:, RWKV_DIM:2 * RWKV_DIM]
    v_r = zs[:, 2 * RWKV_DIM:3 * RWKV_DIM]
    lo = zs[:, 3 * RWKV_DIM:3 * RWKV_DIM + LANES]
    g_lo = zs[:, 3 * RWKV_DIM + LANES:]
    lane = lax.broadcasted_iota(jnp.int32, lo.shape, 1)
    lo = jnp.where(lane < DECAY_LORA, jnp.tanh(lo), lo).astype(_BF16)
    wa = _dot(lo, wwa_ref[...])
    w_pre = w0_ref[...] + wa[:, :RWKV_DIM]
    neg = -w_pre
    softplus = jnp.maximum(neg, 0.0) + jnp.log(1.0 + jnp.exp(-jnp.abs(neg)))
    log_decay = -jnp.exp(-softplus - 0.5)
    a = jax.nn.sigmoid(a0_ref[...] + wa[:, RWKV_DIM:])
    g = _dot(jax.nn.sigmoid(g_lo).astype(_BF16), g2_ref[...])
    kk = k_raw * kk_ref[...]
    kkn = kk / jnp.maximum(jnp.sqrt(_head_sums(kk * kk)), 1e-12)
    r_out[0] = r
    ld_out[0] = log_decay
    kmod_out[0] = k_raw * (1.0 + (a - 1.0) * ka_ref[...])
    vr_out[0] = v_r
    kkn_out[0] = kkn
    b_out[0] = kkn * a
    g_out[0] = g


def _stage_a(x, cosq, sinq, cos_t, sin_t, gmix, win, gq, wq, wqrot, gkv, wk, wv, mu, w0, wwa, a0,
             g2, kk, ka):
    b, s, _ = x.shape
    tm = min(TM_A, s)
    tok = lambda bi, si: (bi, si, 0)
    tok_t = lambda bi, si: (bi, 0, si)
    flat = lambda bi, si: (bi * (s // tm) + si, 0)
    flat_t = lambda bi, si: (0, bi * (s // tm) + si)
    rwkv_tok = jax.ShapeDtypeStruct((b, s, RWKV_DIM), _F32)
    rwkv_spec = pl.BlockSpec((1, tm, RWKV_DIM), tok)
    weights = (gmix, win, gq, wq, wqrot, gkv, wk, wv, mu, w0, wwa, a0, g2, kk, ka)
    return pl.pallas_call(
        _stage_a_kernel,
        grid=(b, s // tm),
        in_specs=[pl.BlockSpec((1, tm, D_MODEL), tok),
                  pl.BlockSpec((tm, HEAD_SLAB), flat),
                  pl.BlockSpec((tm, HEAD_SLAB), flat),
                  pl.BlockSpec((cos_t.shape[0], tm), flat_t),
                  pl.BlockSpec((sin_t.shape[0], tm), flat_t)]
                 + [_resident(w.shape) for w in weights],
        out_specs=[pl.BlockSpec((1, MLA_HEADS * HEAD_SLAB, tm), tok_t),
                   pl.BlockSpec((1, tm, MLA_HEADS * HEAD_SLAB), tok),
                   pl.BlockSpec((1, MLA_HEADS * V_HEAD_DIM, tm), tok_t)]
                  + [rwkv_spec] * 7,
        out_shape=[jax.ShapeDtypeStruct((b, MLA_HEADS * HEAD_SLAB, s), _BF16),
                   jax.ShapeDtypeStruct((b, s, MLA_HEADS * HEAD_SLAB), _BF16),
                   jax.ShapeDtypeStruct((b, MLA_HEADS * V_HEAD_DIM, s), _BF16)] + [rwkv_tok] * 7,
        scratch_shapes=[pltpu.VMEM((1, RWKV_COLS), _F32)],
        compiler_params=pltpu.CompilerParams(
            dimension_semantics=("arbitrary", "arbitrary"), vmem_limit_bytes=VMEM_LIMIT),
        name="stage_a",
    )(x, cosq, sinq, cos_t, sin_t, *weights)


def _attn_kernel(q_ref, k_ref, vt_ref, o_ref, s_a, s_b, mx_a, mx_b, m_ref, acc_ref):
    tq = q_ref.shape[2]
    nh = q_ref.shape[1] // HEAD_SLAB
    qi = pl.program_id(2)
    key = lax.broadcasted_iota(jnp.int32, (tq, tq), 0)
    qry = lax.broadcasted_iota(jnp.int32, (tq, tq), 1)
    causal = key <= qry
    ones = jnp.ones((ATTN_SUM_ROWS, tq), _BF16)

    m_ref[...] = jnp.full(m_ref.shape, -1e30, _F32)
    acc_ref[...] = jnp.zeros(acc_ref.shape, _F32)

    def scores(kt, s_buf, mx_buf, masked=False):
        off = pl.multiple_of(kt * tq, tq)
        for hh in range(nh):
            s = _dot(k_ref[0, pl.ds(off, tq), hh * HEAD_SLAB:(hh + 1) * HEAD_SLAB],
                     q_ref[0, hh * HEAD_SLAB:(hh + 1) * HEAD_SLAB, :])
            if masked:
                s = jnp.where(causal, s, -1e30)
            s_buf[hh] = s
            mx_buf[hh] = jnp.max(s, axis=0, keepdims=True)

    def consume(kt, s_buf, mx_buf):
        off = pl.multiple_of(kt * tq, tq)
        for hh in range(nh):
            m = m_ref[hh]
            m_new = jnp.maximum(m, mx_buf[hh])
            p = jnp.exp2(s_buf[hh] - m_new).astype(_BF16)
            vt = vt_ref[0, hh * V_HEAD_DIM:(hh + 1) * V_HEAD_DIM, pl.ds(off, tq)]
            acc_ref[hh] = (jnp.exp2(m - m_new) * acc_ref[hh]
                           + _dot(jnp.concatenate([vt, ones], axis=0), p))
            m_ref[hh] = m_new

    def tile_at(i):
        return jnp.where(i == 0, qi, i - 1)

    scores(qi, s_a, mx_a, masked=True)

    def two_steps(i2, _):
        i = 2 * i2
        scores(i, s_b, mx_b)
        consume(tile_at(i), s_a, mx_a)
        scores(i + 1, s_a, mx_a)
        consume(i, s_b, mx_b)
        return 0

    lax.fori_loop(0, qi // 2, two_steps, 0)

    @pl.when(qi % 2 == 1)
    def _():
        scores(qi - 1, s_b, mx_b)
        consume(tile_at(qi - 1), s_a, mx_a)
        consume(qi - 1, s_b, mx_b)

    @pl.when(qi % 2 == 0)
    def _():
        consume(tile_at(qi), s_a, mx_a)

    outs = [acc_ref[hh, :V_HEAD_DIM] / acc_ref[hh, V_HEAD_DIM:V_HEAD_DIM + 1] for hh in range(nh)]
    o_ref[0] = jnp.concatenate(outs, axis=0).T.astype(_BF16)


def _attention(qt, k, vt):
    b, s, _ = k.shape
    tq = min(TQ, s)
    groups = MLA_HEADS // ATTN_HEADS
    qk_w = ATTN_HEADS * HEAD_SLAB
    v_w = ATTN_HEADS * V_HEAD_DIM
    return pl.pallas_call(
        _attn_kernel,
        grid=(b, groups, s // tq),
        in_specs=[pl.BlockSpec((1, qk_w, tq), lambda bi, hg, qi: (bi, hg, qi)),
                  pl.BlockSpec((1, s, qk_w), lambda bi, hg, qi: (bi, 0, hg),
                               pipeline_mode=pl.Buffered(1)),
                  pl.BlockSpec((1, v_w, s), lambda bi, hg, qi: (bi, hg, 0),
                               pipeline_mode=pl.Buffered(1))],
        out_specs=pl.BlockSpec((1, tq, v_w), lambda bi, hg, qi: (bi, qi, hg)),
        out_shape=jax.ShapeDtypeStruct((b, s, MLA_HEADS * V_HEAD_DIM), _BF16),
        scratch_shapes=[pltpu.VMEM((ATTN_HEADS, tq, tq), _F32),
                        pltpu.VMEM((ATTN_HEADS, tq, tq), _F32),
                        pltpu.VMEM((ATTN_HEADS, 1, tq), _F32),
                        pltpu.VMEM((ATTN_HEADS, 1, tq), _F32),
                        pltpu.VMEM((ATTN_HEADS, 1, tq), _F32),
                        pltpu.VMEM((ATTN_HEADS, V_HEAD_DIM + ATTN_SUM_ROWS, tq), _F32)],
        compiler_params=pltpu.CompilerParams(
            dimension_semantics=("parallel", "parallel", "arbitrary"),
            vmem_limit_bytes=VMEM_LIMIT),
        name="mla_attention",
    )(qt, k, vt)


def _bdot(a, b):
    return jnp.einsum("nij,njk->nik", a, b, preferred_element_type=_F32)


def _bdot_nt(a, b):
    return jnp.einsum("nik,njk->nij", a, b, preferred_element_type=_F32)


def _bdot_tn(a, b):
    return jnp.einsum("nki,nkj->nij", a, b, preferred_element_type=_F32)


def _wkv_chunk_terms(r, ld, k, v, kk, b):
    n, c, gw = r.shape
    heads = gw // RWKV_HEAD_DIM
    row = lax.broadcasted_iota(jnp.int32, (c, gw), 0)
    col = lax.broadcasted_iota(jnp.int32, (c, gw), 1) % c
    strict = (col < row)[None]
    incl = (col <= row)[None]
    diff = (row ^ col)[None]
    same_head = _same_head((gw, gw))[None]

    def blockdiag(x):
        return jnp.where(same_head, jnp.concatenate([x] * heads, axis=1), jnp.zeros((), x.dtype))

    trow = lax.broadcasted_iota(jnp.int32, (c, c), 0)
    tcol = lax.broadcasted_iota(jnp.int32, (c, c), 1)
    tri = jnp.broadcast_to((tcol <= trow).astype(_BF16)[None], (n, c, c))
    ld_hi = ld.astype(_BF16)
    ld_lo = (ld - ld_hi.astype(_F32)).astype(_BF16)
    cum = _bdot(tri, ld_hi) + _bdot(tri, ld_lo)
    cum_last = cum[:, c - 1:c, :]
    e_neg = jnp.exp(-cum)
    e_tail = jnp.exp(cum_last - cum)
    r_hat = r * jnp.exp(cum)
    a_hat_bf = (-kk * jnp.exp(cum - ld)).astype(_BF16)
    lhs = jnp.concatenate([a_hat_bf, r_hat.astype(_BF16)], axis=1)
    v_bf = v.astype(_BF16)
    a_b = _bdot_nt(lhs, blockdiag((b * e_neg).astype(_BF16)))
    a_k = _bdot_nt(lhs, blockdiag((k * e_neg).astype(_BF16)))
    a_ab = jnp.where(strict, a_b[:, :c], 0.0)
    a_rb = jnp.where(incl, a_b[:, c:], 0.0).astype(_BF16)
    a_akrk = jnp.concatenate([jnp.where(strict, a_k[:, :c], 0.0),
                              jnp.where(incl, a_k[:, c:], 0.0)], axis=1).astype(_BF16)

    t = jnp.where(diff == 0, 1.0, 0.0) + jnp.where(diff == 1, a_ab, 0.0)
    size = 2
    while size < c:
        a_l = jnp.where((diff >= size) & (diff < 2 * size), a_ab, 0.0).astype(_BF16)
        t_bf = t.astype(_BF16)
        t = t + _bdot(_bdot(t_bf, blockdiag(a_l)).astype(_BF16), blockdiag(t_bf))
        size *= 2
    t_bf = t.astype(_BF16)

    v_part = _bdot(a_akrk, blockdiag(v_bf))
    w = _bdot(t_bf, blockdiag(a_hat_bf)).astype(_BF16)
    u0 = _bdot(t_bf, blockdiag(v_part[:, :c].astype(_BF16))).astype(_BF16)
    qe = r_hat + _bdot(a_rb, blockdiag(w))
    y0 = v_part[:, c:] + _bdot(a_rb, blockdiag(u0))
    b_bar = (b * e_tail).astype(_BF16)
    g = jnp.where(same_head, _bdot_tn(w, b_bar), 0.0)
    h = jnp.where(same_head,
                  _bdot_tn(jnp.concatenate([v_bf, u0], axis=1),
                           jnp.concatenate([(k * e_tail).astype(_BF16), b_bar], axis=1)), 0.0)
    return qe.astype(_BF16), y0, g.astype(_BF16), h, jnp.exp(cum_last)


def _wkv_kernel(r_ref, ld_ref, k_ref, v_ref, kk_ref, b_ref, g_ref, rk_ref, lnw_ref, lnb_ref,
                o_ref, state_ref):
    blk, dim = r_ref.shape[1:]
    c = WKV_CHUNK
    nc = blk // c
    groups = dim // WKV_GROUP

    @pl.when(pl.program_id(1) == 0)
    def _():
        state_ref[...] = jnp.zeros_like(state_ref)

    def problems(ref):
        x = ref[0].reshape(nc, c, dim)
        return jnp.concatenate([x[:, :, g0:g0 + WKV_GROUP] for g0 in range(0, dim, WKV_GROUP)],
                               axis=0)

    terms = _wkv_chunk_terms(problems(r_ref), problems(ld_ref), problems(k_ref), problems(v_ref),
                             problems(kk_ref), problems(b_ref))
    qe, y0, g, h, decay = (x.reshape((groups, nc) + x.shape[1:]) for x in terms)

    state = state_ref[...]
    ys = []
    for ci in range(nc):
        state_bf = state.astype(_BF16)
        ys.append(_bdot_nt(qe[:, ci], state_bf) + y0[:, ci])
        state = state * decay[:, ci] + _bdot(state_bf, g[:, ci]) + h[:, ci]
    state_ref[...] = state
    y = jnp.concatenate(ys, axis=1)
    y = jnp.concatenate([y[gi] for gi in range(groups)], axis=-1)

    inv_n = 1.0 / RWKV_HEAD_DIM
    centered = y - _head_sums(y) * inv_n
    var = _head_sums(centered * centered) * inv_n
    yn = centered * lax.rsqrt(var + GN_EPS) * lnw_ref[...] + lnb_ref[...]
    bonus = _head_sums(r_ref[0] * k_ref[0] * rk_ref[...]) * v_ref[0]
    o_ref[0] = ((yn + bonus) * g_ref[0]).astype(_BF16)


def _wkv(r, ld, k, v, kk, bvec, g, rk, lnw, lnb):
    b, s, dim = r.shape
    blk = min(WKV_BLOCK, s)
    tok_spec = pl.BlockSpec((1, blk, dim), lambda bi, si: (bi, si, 0))
    return pl.pallas_call(
        _wkv_kernel,
        grid=(b, s // blk),
        in_specs=[tok_spec] * 7 + [_resident(rk.shape), _resident(lnw.shape), _resident(lnb.shape)],
        out_specs=tok_spec,
        out_shape=jax.ShapeDtypeStruct((b, s, dim), _BF16),
        scratch_shapes=[pltpu.VMEM((dim // WKV_GROUP, WKV_GROUP, WKV_GROUP), _F32)],
        compiler_params=pltpu.CompilerParams(
            dimension_semantics=("parallel", "arbitrary"), vmem_limit_bytes=VMEM_LIMIT),
        name="wkv7",
    )(r, ld, k, v, kk, bvec, g, rk, lnw, lnb)


def _stage_d_kernel(x_ref, oa_ref, ob_ref, p_ref, gmix_ref, wgate_ref, woa_ref, wob_ref, wout_ref,
                    gffn_ref, wup_ref, wdown_ref, gple_ref, wpg_ref, wpp_ref, gfin_ref, out_ref, *,
                    final_norm):
    x = x_ref[...]
    h = _rms(x, gmix_ref[...]).astype(_BF16)
    gate = jax.nn.sigmoid(_dot(h, wgate_ref[...]))
    y_a = _dot(oa_ref[...], woa_ref[...])
    y_b = _dot(ob_ref[...], wob_ref[...])
    mix = gate[:, :D_MODEL] * y_a + gate[:, D_MODEL:] * y_b
    x = x + _dot(mix.astype(_BF16), wout_ref[...])
    h = _rms(x, gffn_ref[...]).astype(_BF16)
    for c0 in range(0, D_FF, FF_CHUNK):
        hid = jnp.square(jnp.maximum(_dot(h, wup_ref[:, c0:c0 + FF_CHUNK]), 0.0))
        x = x + _dot(hid.astype(_BF16), wdown_ref[c0:c0 + FF_CHUNK, :])
    ple_gate = jax.nn.sigmoid(_dot(_rms(x, gple_ref[...]).astype(_BF16), wpg_ref[...]))
    x = x + ple_gate * _dot(p_ref[...].astype(_BF16), wpp_ref[...])
    out_ref[...] = _rms(x, gfin_ref[...]) if final_norm else x


def _stage_d(x, oa, ob, p, gmix, wgate, woa, wob, wout, gffn, wup, wdown, gple, wpg, wpp, gfin,
             final_norm):
    t = x.shape[0]
    tm = min(TM_D, t)
    tok = lambda i: (i, 0)
    weights = (gmix, wgate, woa, wob, wout, gffn, wup, wdown, gple, wpg, wpp, gfin)
    return pl.pallas_call(
        functools.partial(_stage_d_kernel, final_norm=final_norm),
        grid=(t // tm,),
        in_specs=[pl.BlockSpec((tm, D_MODEL), tok),
                  pl.BlockSpec((tm, oa.shape[1]), tok),
                  pl.BlockSpec((tm, ob.shape[1]), tok),
                  pl.BlockSpec((tm, PLE_DIM), tok)] + [_resident(w.shape) for w in weights],
        out_specs=pl.BlockSpec((tm, D_MODEL), tok),
        out_shape=jax.ShapeDtypeStruct((t, D_MODEL), _F32),
        compiler_params=pltpu.CompilerParams(
            dimension_semantics=("parallel",), vmem_limit_bytes=VMEM_LIMIT),
        name="stage_d",
    )(x, oa, ob, p, *weights)


def _head_slabs(w, width, pieces):
    cols = []
    for hd in range(MLA_HEADS):
        used = 0
        for piece in pieces:
            if isinstance(piece, int):
                cols.append(jnp.zeros((w.shape[0], piece), w.dtype))
                used += piece
            else:
                start, stop, sign = piece
                cols.append(sign * w[:, hd * width + start:hd * width + stop])
                used += stop - start
        assert used == HEAD_SLAB
    return jnp.concatenate(cols, axis=1)


def _prepare_layer(i, g_mix, w_in, g_q_a, w_uq, g_kv_a, w_ukv, w_o_mla, mu_rwkv, w0, w2, a0, a2,
                   g2, k_k, k_a, r_k, ln_x_w, ln_x_b, w_o_rwkv, w_out, g_ffn, w_ffn_up, w_ffn_down,
                   g_ple, w_ple_gate, w_ple_proj):
    half = QK_ROPE_DIM // 2
    row = lambda a: a[i].reshape(1, -1)
    w = w_in[i]
    d = w.shape[0]
    kpe = w[:, Q_LORA_RANK + KV_LORA_RANK:MLA_COLS]
    zeros = lambda n: jnp.zeros((d, n), w.dtype)
    kpe_slab = jnp.concatenate([zeros(QK_NOPE_DIM), kpe, zeros(HEAD_SLAB - QK_HEAD_DIM)], axis=1)
    kpe_rot = jnp.concatenate([zeros(QK_NOPE_DIM), -kpe[:, half:], kpe[:, :half],
                               zeros(HEAD_SLAB - QK_HEAD_DIM)], axis=1)
    win = jnp.concatenate([w[:, :Q_LORA_RANK + KV_LORA_RANK], kpe_slab, kpe_rot,
                           w[:, MLA_COLS:MLA_COLS + RWKV_COLS]], axis=1).astype(_BF16)
    wgate = w[:, MLA_COLS + RWKV_COLS:].astype(_BF16)
    pad = HEAD_SLAB - QK_HEAD_DIM
    wq = _head_slabs(w_uq[i], QK_HEAD_DIM, [(0, QK_HEAD_DIM, 1.0), pad]).astype(_BF16).T
    wqrot = _head_slabs(w_uq[i], QK_HEAD_DIM,
                        [QK_NOPE_DIM, (QK_NOPE_DIM + half, QK_HEAD_DIM, -1.0),
                         (QK_NOPE_DIM, QK_NOPE_DIM + half, 1.0), pad]).astype(_BF16).T
    kv_width = QK_NOPE_DIM + V_HEAD_DIM
    wk = _head_slabs(w_ukv[i], kv_width, [(0, QK_NOPE_DIM, 1.0), HEAD_SLAB - QK_NOPE_DIM]).astype(_BF16)
    wv = jnp.concatenate([w_ukv[i][:, hd * kv_width + QK_NOPE_DIM:(hd + 1) * kv_width]
                          for hd in range(MLA_HEADS)], axis=1).astype(_BF16).T
    wwa = jnp.concatenate(
        [jnp.concatenate([w2[i], jnp.zeros_like(w2[i])], axis=1),
         jnp.concatenate([jnp.zeros_like(a2[i]), a2[i]], axis=1)], axis=0).astype(_BF16)
    stage_a = (row(g_mix), win, row(g_q_a), wq, wqrot, row(g_kv_a), wk, wv, row(mu_rwkv), row(w0),
               wwa, row(a0), g2[i].astype(_BF16), row(k_k), row(k_a))
    wkv = (r_k[i].reshape(1, -1), row(ln_x_w), row(ln_x_b))
    stage_d = (row(g_mix), wgate, w_o_mla[i].astype(_BF16), w_o_rwkv[i].astype(_BF16),
               w_out[i].astype(_BF16), row(g_ffn), w_ffn_up[i].astype(_BF16),
               w_ffn_down[i].astype(_BF16), row(g_ple), w_ple_gate[i].astype(_BF16),
               w_ple_proj[i].astype(_BF16))
    return stage_a, wkv, stage_d


def kernel(x, p, positions, g_mix, w_in, g_q_a, w_uq, g_kv_a, w_ukv, w_o_mla, mu_rwkv, w0, w2, a0, a2, g2, k_k, k_a, r_k, ln_x_w, ln_x_b, w_o_rwkv, w_out, g_ffn, w_ffn_up, w_ffn_down, g_ple, w_ple_gate, w_ple_proj, g_final):
    b, s, d = x.shape
    depth = w_in.shape[0]
    t = b * s
    cos_t, sin_t = _rope_tables(positions)
    periods = HEAD_SLAB // cos_t.shape[0]
    cosk = jnp.tile(cos_t.T, (1, periods))
    sink = jnp.tile(sin_t.T, (1, periods))
    gfin = g_final.reshape(1, -1)
    for i in range(depth):
        sa, sw, sd = _prepare_layer(i, g_mix, w_in, g_q_a, w_uq, g_kv_a, w_ukv, w_o_mla, mu_rwkv,
                                    w0, w2, a0, a2, g2, k_k, k_a, r_k, ln_x_w, ln_x_b, w_o_rwkv,
                                    w_out, g_ffn, w_ffn_up, w_ffn_down, g_ple, w_ple_gate,
                                    w_ple_proj)
        q, k, v, r, ld, kmod, vr, kkn, bvec, g = _stage_a(x, cosk, sink, cos_t, sin_t, *sa)
        o_a = _attention(q, k, v)
        o_b = _wkv(r, ld, kmod, vr, kkn, bvec, g, *sw)
        out = _stage_d(x.reshape(t, d), o_a.reshape(t, -1), o_b.reshape(t, -1), p[i].reshape(t, -1),
                       *sd, gfin, final_norm=(i == depth - 1))
        x = out.reshape(b, s, d)
    return x
```

```python
import functools

import jax
import jax.numpy as jnp
from jax import lax
from jax.experimental import pallas as pl
from jax.experimental.pallas import tpu as pltpu

D_MODEL = 1024
MLA_HEADS = 8
QK_NOPE_DIM = 64
QK_ROPE_DIM = 32
QK_HEAD_DIM = QK_NOPE_DIM + QK_ROPE_DIM
V_HEAD_DIM = 64
Q_LORA_RANK = 384
KV_LORA_RANK = 256
ROPE_THETA = 10000.0
RWKV_HEADS = 8
RWKV_HEAD_DIM = 64
RWKV_DIM = RWKV_HEADS * RWKV_HEAD_DIM
DECAY_LORA = 64
AAA_LORA = 64
GATE_LORA = 128
GN_EPS = RWKV_HEAD_DIM * 1e-5
MLA_COLS = Q_LORA_RANK + KV_LORA_RANK + QK_ROPE_DIM
RWKV_COLS = 3 * RWKV_DIM + DECAY_LORA + AAA_LORA + GATE_LORA
D_FF = 4 * D_MODEL
PLE_DIM = 256
RMS_EPS = 1e-6
LOG2_E = 1.4426950408889634

LANES = 128
HEAD_SLAB = LANES
VMEM_LIMIT = 56 * 1024 * 1024

TM_A = 512
TQ = 256
ATTN_HEADS = 8
ATTN_SUM_ROWS = 16
WKV_CHUNK = 64
WKV_BLOCK = 512
WKV_GROUP = 256
TM_D = 512
FF_CHUNK = 1024

A_CQ = 0
A_CKV = A_CQ + Q_LORA_RANK
A_KPE = A_CKV + KV_LORA_RANK
A_KPE_ROT = A_KPE + HEAD_SLAB
A_RWKV = A_KPE_ROT + HEAD_SLAB
A_COLS = A_RWKV + RWKV_COLS

_BF16 = jnp.bfloat16
_F32 = jnp.float32


def _dot(a, b):
    return jnp.dot(a, b, preferred_element_type=_F32)


def _dot_nt(a, b):
    return lax.dot_general(a, b, (((1,), (1,)), ((), ())), preferred_element_type=_F32)


def _dot_tn(a, b):
    return lax.dot_general(a, b, (((0,), (0,)), ((), ())), preferred_element_type=_F32)


def _rms(x, g):
    return x * lax.rsqrt(jnp.mean(x * x, axis=-1, keepdims=True) + RMS_EPS) * g


def _same_head(shape):
    row = lax.broadcasted_iota(jnp.int32, shape, len(shape) - 2) // RWKV_HEAD_DIM
    col = lax.broadcasted_iota(jnp.int32, shape, len(shape) - 1) // RWKV_HEAD_DIM
    return row == col


def _head_sums(x):
    ones_bd = jnp.where(_same_head((WKV_GROUP, WKV_GROUP)), 1.0, 0.0).astype(_BF16)
    xb = x.astype(_BF16)
    return jnp.concatenate([_dot(xb[:, g0:g0 + WKV_GROUP], ones_bd)
                            for g0 in range(0, x.shape[1], WKV_GROUP)], axis=1)


def _resident(shape):
    zeros = (0,) * len(shape)
    return pl.BlockSpec(shape, lambda *_: zeros, pipeline_mode=pl.Buffered(1))


def _rope_kernel(pos_ref, inv_ref, cos_ref, sin_ref):
    ang = pos_ref[...] * inv_ref[...]
    cos_ref[...] = jnp.cos(ang)
    sin_ref[...] = jnp.sin(ang)


def _rope_tables(positions):
    half = QK_ROPE_DIM // 2
    t = positions.size
    inv_freq = ROPE_THETA ** (-jnp.arange(half, dtype=_F32) / half)
    return pl.pallas_call(
        _rope_kernel,
        out_shape=(jax.ShapeDtypeStruct((half, t), _F32),) * 2,
        name="rope_tables",
    )(positions.reshape(1, t).astype(_F32), inv_freq.reshape(half, 1))


def _stage_a_kernel(x_ref, cosq_ref, sinq_ref, cos_t_ref, sin_t_ref,
                    gmix_ref, win_ref, gq_ref, wq_ref, wqrot_ref,
                    gkv_ref, wk_ref, wv_ref, mu_ref, w0_ref, wwa_ref, a0_ref, g2_ref,
                    kk_ref, ka_ref,
                    q_out, k_out, v_out, r_out, ld_out, kmod_out, vr_out, kkn_out, b_out, g_out,
                    carry_ref):
    tm = x_ref.shape[1]

    @pl.when(pl.program_id(1) == 0)
    def _():
        carry_ref[...] = jnp.zeros_like(carry_ref)

    h = _rms(x_ref[0], gmix_ref[...]).astype(_BF16)
    z = _dot(h, win_ref[...])

    cos2 = jnp.concatenate([cos_t_ref[...]] * 2, axis=0)
    sin2 = jnp.concatenate([sin_t_ref[...]] * 2, axis=0)
    cq = _rms(z[:, A_CQ:A_CQ + Q_LORA_RANK], gq_ref[...]).astype(_BF16)
    q_all = _dot_nt(wq_ref[...], cq)
    q_rot = _dot_nt(wqrot_ref[...], cq)
    q_scale = QK_HEAD_DIM ** -0.5 * LOG2_E
    pad = jnp.zeros((HEAD_SLAB - QK_HEAD_DIM, tm), _BF16)
    rows = []
    for hd in range(MLA_HEADS):
        base = hd * QK_HEAD_DIM
        nope = q_all[base:base + QK_NOPE_DIM]
        rope = (q_all[base + QK_NOPE_DIM:base + QK_HEAD_DIM] * cos2
                + q_rot[hd * QK_ROPE_DIM:(hd + 1) * QK_ROPE_DIM] * sin2)
        rows += [(nope * q_scale).astype(_BF16), (rope * q_scale).astype(_BF16), pad]
    q_out[0] = jnp.concatenate(rows, axis=0)
    ckv = _rms(z[:, A_CKV:A_CKV + KV_LORA_RANK], gkv_ref[...]).astype(_BF16)
    kpe = (z[:, A_KPE:A_KPE + HEAD_SLAB] * cosq_ref[...]
           + z[:, A_KPE_ROT:A_KPE_ROT + HEAD_SLAB] * sinq_ref[...])
    k = _dot(ckv, wk_ref[...]) + jnp.concatenate([kpe] * MLA_HEADS, axis=1)
    k_out[0] = k.astype(_BF16)
    v_out[0] = _dot_nt(wv_ref[...], ckv).astype(_BF16)

    zr = z[:, A_RWKV:A_RWKV + RWKV_COLS]
    row = lax.broadcasted_iota(jnp.int32, zr.shape, 0)
    prev = jnp.where(row == 0, carry_ref[...], pltpu.roll(zr, shift=1, axis=0))
    carry_ref[...] = zr[tm - 1:tm, :]
    zs = zr + (prev - zr) * mu_ref[...]
    r = zs[:, 0:RWKV_DIM]
    k_raw = zs[:, RWKV_DIM:2 * RWKV_DIM]
    v_r = zs[:, 2 * RWKV_DIM:3 * RWKV_DIM]
    lo = zs[:, 3 * RWKV_DIM:3 * RWKV_DIM + LANES]
    g_lo = zs[:, 3 * RWKV_DIM + LANES:]
    lane = lax.broadcasted_iota(jnp.int32, lo.shape, 1)
    lo = jnp.where(lane < DECAY_LORA, jnp.tanh(lo), lo).astype(_BF16)
    wa = _dot(lo, wwa_ref[...])
    w_pre = w0_ref[...] + wa[:, :RWKV_DIM]
    neg = -w_pre
    softplus = jnp.maximum(neg, 0.0) + jnp.log(1.0 + jnp.exp(-jnp.abs(neg)))
    log_decay = -jnp.exp(-softplus - 0.5)
    a = jax.nn.sigmoid(a0_ref[...] + wa[:, RWKV_DIM:])
    g = _dot(jax.nn.sigmoid(g_lo).astype(_BF16), g2_ref[...])
    kk = k_raw * kk_ref[...]
    kkn = kk / jnp.maximum(jnp.sqrt(_head_sums(kk * kk)), 1e-12)
    r_out[0] = r
    ld_out[0] = log_decay
    kmod_out[0] = k_raw * (1.0 + (a - 1.0) * ka_ref[...])
    vr_out[0] = v_r
    kkn_out[0] = kkn
    b_out[0] = kkn * a
    g_out[0] = g


def _stage_a(x, cosq, sinq, cos_t, sin_t, gmix, win, gq, wq, wqrot, gkv, wk, wv, mu, w0, wwa, a0,
             g2, kk, ka):
    b, s, _ = x.shape
    tm = min(TM_A, s)
    tok = lambda bi, si: (bi, si, 0)
    tok_t = lambda bi, si: (bi, 0, si)
    flat = lambda bi, si: (bi * (s // tm) + si, 0)
    flat_t = lambda bi, si: (0, bi * (s // tm) + si)
    rwkv_tok = jax.ShapeDtypeStruct((b, s, RWKV_DIM), _F32)
    rwkv_spec = pl.BlockSpec((1, tm, RWKV_DIM), tok)
    weights = (gmix, win, gq, wq, wqrot, gkv, wk, wv, mu, w0, wwa, a0, g2, kk, ka)
    return pl.pallas_call(
        _stage_a_kernel,
        grid=(b, s // tm),
        in_specs=[pl.BlockSpec((1, tm, D_MODEL), tok),
                  pl.BlockSpec((tm, HEAD_SLAB), flat),
                  pl.BlockSpec((tm, HEAD_SLAB), flat),
                  pl.BlockSpec((cos_t.shape[0], tm), flat_t),
                  pl.BlockSpec((sin_t.shape[0], tm), flat_t)]
                 + [_resident(w.shape) for w in weights],
        out_specs=[pl.BlockSpec((1, MLA_HEADS * HEAD_SLAB, tm), tok_t),
                   pl.BlockSpec((1, tm, MLA_HEADS * HEAD_SLAB), tok),
                   pl.BlockSpec((1, MLA_HEADS * V_HEAD_DIM, tm), tok_t)]
                  + [rwkv_spec] * 7,
        out_shape=[jax.ShapeDtypeStruct((b, MLA_HEADS * HEAD_SLAB, s), _BF16),
                   jax.ShapeDtypeStruct((b, s, MLA_HEADS * HEAD_SLAB), _BF16),
                   jax.ShapeDtypeStruct((b, MLA_HEADS * V_HEAD_DIM, s), _BF16)] + [rwkv_tok] * 7,
        scratch_shapes=[pltpu.VMEM((1, RWKV_COLS), _F32)],
        compiler_params=pltpu.CompilerParams(
            dimension_semantics=("arbitrary", "arbitrary"), vmem_limit_bytes=VMEM_LIMIT),
        name="stage_a",
    )(x, cosq, sinq, cos_t, sin_t, *weights)


def _attn_kernel(q_ref, k_ref, vt_ref, o_ref, s_a, s_b, mx_a, mx_b, m_ref, acc_ref):
    tq = q_ref.shape[2]
    nh = q_ref.shape[1] // HEAD_SLAB
    qi = pl.program_id(2)
    key = lax.broadcasted_iota(jnp.int32, (tq, tq), 0)
    qry = lax.broadcasted_iota(jnp.int32, (tq, tq), 1)
    causal = key <= qry
    ones = jnp.ones((ATTN_SUM_ROWS, tq), _BF16)

    m_ref[...] = jnp.full(m_ref.shape, -1e30, _F32)
    acc_ref[...] = jnp.zeros(acc_ref.shape, _F32)

    def scores(kt, s_buf, mx_buf, masked=False):
        off = pl.multiple_of(kt * tq, tq)
        for hh in range(nh):
            s = _dot(k_ref[0, pl.ds(off, tq), hh * HEAD_SLAB:(hh + 1) * HEAD_SLAB],
                     q_ref[0, hh * HEAD_SLAB:(hh + 1) * HEAD_SLAB, :])
            if masked:
                s = jnp.where(causal, s, -1e30)
            s_buf[hh] = s
            mx_buf[hh] = jnp.max(s, axis=0, keepdims=True)

    def consume(kt, s_buf, mx_buf):
        off = pl.multiple_of(kt * tq, tq)
        for hh in range(nh):
            m = m_ref[hh]
            m_new = jnp.maximum(m, mx_buf[hh])
            p = jnp.exp2(s_buf[hh] - m_new).astype(_BF16)
            vt = vt_ref[0, hh * V_HEAD_DIM:(hh + 1) * V_HEAD_DIM, pl.ds(off, tq)]
            acc_ref[hh] = (jnp.exp2(m - m_new) * acc_ref[hh]
                           + _dot(jnp.concatenate([vt, ones], axis=0), p))
            m_ref[hh] = m_new

    def tile_at(i):
        return jnp.where(i == 0, qi, i - 1)

    scores(qi, s_a, mx_a, masked=True)

    def two_steps(i2, _):
        i = 2 * i2
        scores(i, s_b, mx_b)
        consume(tile_at(i), s_a, mx_a)
        scores(i + 1, s_a, mx_a)
        consume(i, s_b, mx_b)
        return 0

    lax.fori_loop(0, qi // 2, two_steps, 0)

    @pl.when(qi % 2 == 1)
    def _():
        scores(qi - 1, s_b, mx_b)
        consume(tile_at(qi - 1), s_a, mx_a)
        consume(qi - 1, s_b, mx_b)

    @pl.when(qi % 2 == 0)
    def _():
        consume(tile_at(qi), s_a, mx_a)

    outs = [acc_ref[hh, :V_HEAD_DIM] / acc_ref[hh, V_HEAD_DIM:V_HEAD_DIM + 1] for hh in range(nh)]
    o_ref[0] = jnp.concatenate(outs, axis=0).T.astype(_BF16)


def _attention(qt, k, vt):
    b, s, _ = k.shape
    tq = min(TQ, s)
    groups = MLA_HEADS // ATTN_HEADS
    qk_w = ATTN_HEADS * HEAD_SLAB
    v_w = ATTN_HEADS * V_HEAD_DIM
    return pl.pallas_call(
        _attn_kernel,
        grid=(b, groups, s // tq),
        in_specs=[pl.BlockSpec((1, qk_w, tq), lambda bi, hg, qi: (bi, hg, qi)),
                  pl.BlockSpec((1, s, qk_w), lambda bi, hg, qi: (bi, 0, hg),
                               pipeline_mode=pl.Buffered(1)),
                  pl.BlockSpec((1, v_w, s), lambda bi, hg, qi: (bi, hg, 0),
                               pipeline_mode=pl.Buffered(1))],
        out_specs=pl.BlockSpec((1, tq, v_w), lambda bi, hg, qi: (bi, qi, hg)),
        out_shape=jax.ShapeDtypeStruct((b, s, MLA_HEADS * V_HEAD_DIM), _BF16),
        scratch_shapes=[pltpu.VMEM((ATTN_HEADS, tq, tq), _F32),
                        pltpu.VMEM((ATTN_HEADS, tq, tq), _F32),
                        pltpu.VMEM((ATTN_HEADS, 1, tq), _F32),
                        pltpu.VMEM((ATTN_HEADS, 1, tq), _F32),
                        pltpu.VMEM((ATTN_HEADS, 1, tq), _F32),
                        pltpu.VMEM((ATTN_HEADS, V_HEAD_DIM + ATTN_SUM_ROWS, tq), _F32)],
        compiler_params=pltpu.CompilerParams(
            dimension_semantics=("parallel", "parallel", "arbitrary"),
            vmem_limit_bytes=VMEM_LIMIT),
        name="mla_attention",
    )(qt, k, vt)


def _bdot(a, b):
    return jnp.einsum("nij,njk->nik", a, b, preferred_element_type=_F32)


def _bdot_nt(a, b):
    return jnp.einsum("nik,njk->nij", a, b, preferred_element_type=_F32)


def _bdot_tn(a, b):
    return jnp.einsum("nki,nkj->nij", a, b, preferred_element_type=_F32)


def _wkv_chunk_terms(r, ld, k, v, kk, b):
    n, c, gw = r.shape
    heads = gw // RWKV_HEAD_DIM
    row = lax.broadcasted_iota(jnp.int32, (c, gw), 0)
    col = lax.broadcasted_iota(jnp.int32, (c, gw), 1) % c
    strict = (col < row)[None]
    incl = (col <= row)[None]
    diff = (row ^ col)[None]
    same_head = _same_head((gw, gw))[None]

    def blockdiag(x):
        return jnp.where(same_head, jnp.concatenate([x] * heads, axis=1), jnp.zeros((), x.dtype))

    trow = lax.broadcasted_iota(jnp.int32, (c, c), 0)
    tcol = lax.broadcasted_iota(jnp.int32, (c, c), 1)
    tri = jnp.broadcast_to((tcol <= trow).astype(_BF16)[None], (n, c, c))
    ld_hi = ld.astype(_BF16)
    ld_lo = (ld - ld_hi.astype(_F32)).astype(_BF16)
    cum = _bdot(tri, ld_hi) + _bdot(tri, ld_lo)
    cum_last = cum[:, c - 1:c, :]
    e_neg = jnp.exp(-cum)
    e_tail = jnp.exp(cum_last - cum)
    r_hat = r * jnp.exp(cum)
    a_hat_bf = (-kk * jnp.exp(cum - ld)).astype(_BF16)
    lhs = jnp.concatenate([a_hat_bf, r_hat.astype(_BF16)], axis=1)
    v_bf = v.astype(_BF16)
    a_b = _bdot_nt(lhs, blockdiag((b * e_neg).astype(_BF16)))
    a_k = _bdot_nt(lhs, blockdiag((k * e_neg).astype(_BF16)))
    a_ab = jnp.where(strict, a_b[:, :c], 0.0)
    a_rb = jnp.where(incl, a_b[:, c:], 0.0).astype(_BF16)
    a_akrk = jnp.concatenate([jnp.where(strict, a_k[:, :c], 0.0),
                              jnp.where(incl, a_k[:, c:], 0.0)], axis=1).astype(_BF16)

    t = jnp.where(diff == 0, 1.0, 0.0) + jnp.where(diff == 1, a_ab, 0.0)
    size = 2
    while size < c:
        a_l = jnp.where((diff >= size) & (diff < 2 * size), a_ab, 0.0).astype(_BF16)
        t_bf = t.astype(_BF16)
        t = t + _bdot(_bdot(t_bf, blockdiag(a_l)).astype(_BF16), blockdiag(t_bf))
        size *= 2
    t_bf = t.astype(_BF16)

    v_part = _bdot(a_akrk, blockdiag(v_bf))
    w = _bdot(t_bf, blockdiag(a_hat_bf)).astype(_BF16)
    u0 = _bdot(t_bf, blockdiag(v_part[:, :c].astype(_BF16))).astype(_BF16)
    qe = r_hat + _bdot(a_rb, blockdiag(w))
    y0 = v_part[:, c:] + _bdot(a_rb, blockdiag(u0))
    b_bar = (b * e_tail).astype(_BF16)
    g = jnp.where(same_head, _bdot_tn(w, b_bar), 0.0)
    h = jnp.where(same_head,
                  _bdot_tn(jnp.concatenate([v_bf, u0], axis=1),
                           jnp.concatenate([(k * e_tail).astype(_BF16), b_bar], axis=1)), 0.0)
    return qe.astype(_BF16), y0, g.astype(_BF16), h, jnp.exp(cum_last)


def _wkv_kernel(r_ref, ld_ref, k_ref, v_ref, kk_ref, b_ref, g_ref, rk_ref, lnw_ref, lnb_ref,
                o_ref, state_ref):
    blk, dim = r_ref.shape[1:]
    c = WKV_CHUNK
    nc = blk // c
    groups = dim // WKV_GROUP

    @pl.when(pl.program_id(1) == 0)
    def _():
        state_ref[...] = jnp.zeros_like(state_ref)

    def problems(ref):
        x = ref[0].reshape(nc, c, dim)
        return jnp.concatenate([x[:, :, g0:g0 + WKV_GROUP] for g0 in range(0, dim, WKV_GROUP)],
                               axis=0)

    terms = _wkv_chunk_terms(problems(r_ref), problems(ld_ref), problems(k_ref), problems(v_ref),
                             problems(kk_ref), problems(b_ref))
    qe, y0, g, h, decay = (x.reshape((groups, nc) + x.shape[1:]) for x in terms)

    state = state_ref[...]
    ys = []
    for ci in range(nc):
        state_bf = state.astype(_BF16)
        ys.append(_bdot_nt(qe[:, ci], state_bf) + y0[:, ci])
        state = state * decay[:, ci] + _bdot(state_bf, g[:, ci]) + h[:, ci]
    state_ref[...] = state
    y = jnp.concatenate(ys, axis=1)
    y = jnp.concatenate([y[gi] for gi in range(groups)], axis=-1)

    inv_n = 1.0 / RWKV_HEAD_DIM
    centered = y - _head_sums(y) * inv_n
    var = _head_sums(centered * centered) * inv_n
    yn = centered * lax.rsqrt(var + GN_EPS) * lnw_ref[...] + lnb_ref[...]
    bonus = _head_sums(r_ref[0] * k_ref[0] * rk_ref[...]) * v_ref[0]
    o_ref[0] = ((yn + bonus) * g_ref[0]).astype(_BF16)


def _wkv(r, ld, k, v, kk, bvec, g, rk, lnw, lnb):
    b, s, dim = r.shape
    blk = min(WKV_BLOCK, s)
    tok_spec = pl.BlockSpec((1, blk, dim), lambda bi, si: (bi, si, 0))
    return pl.pallas_call(
        _wkv_kernel,
        grid=(b, s // blk),
        in_specs=[tok_spec] * 7 + [_resident(rk.shape), _resident(lnw.shape), _resident(lnb.shape)],
        out_specs=tok_spec,
        out_shape=jax.ShapeDtypeStruct((b, s, dim), _BF16),
        scratch_shapes=[pltpu.VMEM((dim // WKV_GROUP, WKV_GROUP, WKV_GROUP), _F32)],
        compiler_params=pltpu.CompilerParams(
            dimension_semantics=("parallel", "arbitrary"), vmem_limit_bytes=VMEM_LIMIT),
        name="wkv7",
    )(r, ld, k, v, kk, bvec, g, rk, lnw, lnb)


def _stage_d_kernel(x_ref, oa_ref, ob_ref, p_ref, gmix_ref, wgate_ref, woa_ref, wob_ref, wout_ref,
                    gffn_ref, wup_ref, wdown_ref, gple_ref, wpg_ref, wpp_ref, gfin_ref, out_ref, *,
                    final_norm):
    x = x_ref[...]
    h = _rms(x, gmix_ref[...]).astype(_BF16)
    gate = jax.nn.sigmoid(_dot(h, wgate_ref[...]))
    y_a = _dot(oa_ref[...], woa_ref[...])
    y_b = _dot(ob_ref[...], wob_ref[...])
    mix = gate[:, :D_MODEL] * y_a + gate[:, D_MODEL:] * y_b
    x = x + _dot(mix.astype(_BF16), wout_ref[...])
    h = _rms(x, gffn_ref[...]).astype(_BF16)
    for c0 in range(0, D_FF, FF_CHUNK):
        hid = jnp.square(jnp.maximum(_dot(h, wup_ref[:, c0:c0 + FF_CHUNK]), 0.0))
        x = x + _dot(hid.astype(_BF16), wdown_ref[c0:c0 + FF_CHUNK, :])
    ple_gate = jax.nn.sigmoid(_dot(_rms(x, gple_ref[...]).astype(_BF16), wpg_ref[...]))
    x = x + ple_gate * _dot(p_ref[...].astype(_BF16), wpp_ref[...])
    out_ref[...] = _rms(x, gfin_ref[...]) if final_norm else x


def _stage_d(x, oa, ob, p, gmix, wgate, woa, wob, wout, gffn, wup, wdown, gple, wpg, wpp, gfin,
             final_norm):
    t = x.shape[0]
    tm = min(TM_D, t)
    tok = lambda i: (i, 0)
    weights = (gmix, wgate, woa, wob, wout, gffn, wup, wdown, gple, wpg, wpp, gfin)
    return pl.pallas_call(
        functools.partial(_stage_d_kernel, final_norm=final_norm),
        grid=(t // tm,),
        in_specs=[pl.BlockSpec((tm, D_MODEL), tok),
                  pl.BlockSpec((tm, oa.shape[1]), tok),
                  pl.BlockSpec((tm, ob.shape[1]), tok),
                  pl.BlockSpec((tm, PLE_DIM), tok)] + [_resident(w.shape) for w in weights],
        out_specs=pl.BlockSpec((tm, D_MODEL), tok),
        out_shape=jax.ShapeDtypeStruct((t, D_MODEL), _F32),
        compiler_params=pltpu.CompilerParams(
            dimension_semantics=("parallel",), vmem_limit_bytes=VMEM_LIMIT),
        name="stage_d",
    )(x, oa, ob, p, *weights)


def _rotate_half(w_pe):
    half = w_pe.shape[-1] // 2
    return jnp.concatenate([-w_pe[..., half:], w_pe[..., :half]], axis=-1)


def _prepare_layer(i, g_mix, w_in, g_q_a, w_uq, g_kv_a, w_ukv, w_o_mla, mu_rwkv, w0, w2, a0, a2,
                   g2, k_k, k_a, r_k, ln_x_w, ln_x_b, w_o_rwkv, w_out, g_ffn, w_ffn_up, w_ffn_down,
                   g_ple, w_ple_gate, w_ple_proj):
    row = lambda a: a[i].reshape(1, -1)
    w = w_in[i].astype(_BF16)
    kpe = w[:, Q_LORA_RANK + KV_LORA_RANK:MLA_COLS]
    slab = lambda cols: jnp.pad(cols, ((0, 0), (QK_NOPE_DIM, HEAD_SLAB - QK_HEAD_DIM)))
    win = jnp.concatenate([w[:, :Q_LORA_RANK + KV_LORA_RANK], slab(kpe), slab(_rotate_half(kpe)),
                           w[:, MLA_COLS:MLA_COLS + RWKV_COLS]], axis=1)
    wgate = w[:, MLA_COLS + RWKV_COLS:]
    uq = w_uq[i].astype(_BF16)
    wq = uq.T
    uq_pe = uq.reshape(-1, MLA_HEADS, QK_HEAD_DIM)[:, :, QK_NOPE_DIM:]
    wqrot = _rotate_half(uq_pe).reshape(-1, MLA_HEADS * QK_ROPE_DIM).T
    ukv = w_ukv[i].astype(_BF16).reshape(-1, MLA_HEADS, QK_NOPE_DIM + V_HEAD_DIM)
    wk = jnp.pad(ukv[:, :, :QK_NOPE_DIM], ((0, 0), (0, 0), (0, HEAD_SLAB - QK_NOPE_DIM))).reshape(
        -1, MLA_HEADS * HEAD_SLAB)
    wv = ukv[:, :, QK_NOPE_DIM:].reshape(-1, MLA_HEADS * V_HEAD_DIM).T
    wwa = jnp.concatenate(
        [jnp.concatenate([w2[i], jnp.zeros_like(w2[i])], axis=1),
         jnp.concatenate([jnp.zeros_like(a2[i]), a2[i]], axis=1)], axis=0).astype(_BF16)
    stage_a = (row(g_mix), win, row(g_q_a), wq, wqrot, row(g_kv_a), wk, wv, row(mu_rwkv), row(w0),
               wwa, row(a0), g2[i].astype(_BF16), row(k_k), row(k_a))
    wkv = (r_k[i].reshape(1, -1), row(ln_x_w), row(ln_x_b))
    stage_d = (row(g_mix), wgate, w_o_mla[i].astype(_BF16), w_o_rwkv[i].astype(_BF16),
               w_out[i].astype(_BF16), row(g_ffn), w_ffn_up[i].astype(_BF16),
               w_ffn_down[i].astype(_BF16), row(g_ple), w_ple_gate[i].astype(_BF16),
               w_ple_proj[i].astype(_BF16))
    return stage_a, wkv, stage_d


def kernel(x, p, positions, g_mix, w_in, g_q_a, w_uq, g_kv_a, w_ukv, w_o_mla, mu_rwkv, w0, w2, a0, a2, g2, k_k, k_a, r_k, ln_x_w, ln_x_b, w_o_rwkv, w_out, g_ffn, w_ffn_up, w_ffn_down, g_ple, w_ple_gate, w_ple_proj, g_final):
    b, s, d = x.shape
    depth = w_in.shape[0]
    t = b * s
    cos_t, sin_t = _rope_tables(positions)
    periods = HEAD_SLAB // cos_t.shape[0]
    cosk = jnp.tile(cos_t.T, (1, periods))
    sink = jnp.tile(sin_t.T, (1, periods))
    gfin = g_final.reshape(1, -1)
    for i in range(depth):
        sa, sw, sd = _prepare_layer(i, g_mix, w_in, g_q_a, w_uq, g_kv_a, w_ukv, w_o_mla, mu_rwkv,
                                    w0, w2, a0, a2, g2, k_k, k_a, r_k, ln_x_w, ln_x_b, w_o_rwkv,
                                    w_out, g_ffn, w_ffn_up, w_ffn_down, g_ple, w_ple_gate,
                                    w_ple_proj)
        q, k, v, r, ld, kmod, vr, kkn, bvec, g = _stage_a(x, cosk, sink, cos_t, sin_t, *sa)
        o_a = _attention(q, k, v)
        o_b = _wkv(r, ld, kmod, vr, kkn, bvec, g, *sw)
        out = _stage_d(x.reshape(t, d), o_a.reshape(t, -1), o_b.reshape(t, -1), p[i].reshape(t, -1),
                       *sd, gfin, final_norm=(i == depth - 1))
        x = out.reshape(b, s, d)
    return x
```

```python
import functools

import jax
import jax.numpy as jnp
from jax import lax
from jax.experimental import pallas as pl
from jax.experimental.pallas import tpu as pltpu

D_MODEL = 1024
MLA_HEADS = 8
QK_NOPE_DIM = 64
QK_ROPE_DIM = 32
QK_HEAD_DIM = QK_NOPE_DIM + QK_ROPE_DIM
V_HEAD_DIM = 64
Q_LORA_RANK = 384
KV_LORA_RANK = 256
ROPE_THETA = 10000.0
RWKV_HEADS = 8
RWKV_HEAD_DIM = 64
RWKV_DIM = RWKV_HEADS * RWKV_HEAD_DIM
DECAY_LORA = 64
AAA_LORA = 64
GATE_LORA = 128
GN_EPS = RWKV_HEAD_DIM * 1e-5
MLA_COLS = Q_LORA_RANK + KV_LORA_RANK + QK_ROPE_DIM
RWKV_COLS = 3 * RWKV_DIM + DECAY_LORA + AAA_LORA + GATE_LORA
D_FF = 4 * D_MODEL
PLE_DIM = 256
RMS_EPS = 1e-6
LOG2_E = 1.4426950408889634

LANES = 128
HEAD_SLAB = LANES
VMEM_LIMIT = 56 * 1024 * 1024

TM_A = 512
TQ = 512
ATTN_HEADS = 8
ATTN_SUM_ROWS = 16
WKV_CHUNK = 64
WKV_BLOCK = 512
WKV_GROUP = 256
TM_D = 512
FF_CHUNK = 1024

A_CQ = 0
A_CKV = A_CQ + Q_LORA_RANK
A_KPE = A_CKV + KV_LORA_RANK
A_RWKV = A_KPE + HEAD_SLAB
A_COLS = A_RWKV + RWKV_COLS

_BF16 = jnp.bfloat16
_F32 = jnp.float32


def _dot(a, b):
    return jnp.dot(a, b, preferred_element_type=_F32)


def _dot_nt(a, b):
    return lax.dot_general(a, b, (((1,), (1,)), ((), ())), preferred_element_type=_F32)


def _dot_tn(a, b):
    return lax.dot_general(a, b, (((0,), (0,)), ((), ())), preferred_element_type=_F32)


def _rms(x, g):
    return x * lax.rsqrt(jnp.mean(x * x, axis=-1, keepdims=True) + RMS_EPS) * g


def _same_head(shape):
    row = lax.broadcasted_iota(jnp.int32, shape, len(shape) - 2) // RWKV_HEAD_DIM
    col = lax.broadcasted_iota(jnp.int32, shape, len(shape) - 1) // RWKV_HEAD_DIM
    return row == col


def _head_sums(x):
    ones_bd = jnp.where(_same_head((WKV_GROUP, WKV_GROUP)), 1.0, 0.0).astype(_BF16)
    xb = x.astype(_BF16)
    return jnp.concatenate([_dot(xb[:, g0:g0 + WKV_GROUP], ones_bd)
                            for g0 in range(0, x.shape[1], WKV_GROUP)], axis=1)


def _resident(shape):
    zeros = (0,) * len(shape)
    return pl.BlockSpec(shape, lambda *_: zeros, pipeline_mode=pl.Buffered(1))


def _rope_kernel(pos_ref, inv_ref, cos_ref, sin_ref):
    ang = pos_ref[...] * inv_ref[...]
    cos_ref[...] = jnp.cos(ang)
    sin_ref[...] = jnp.sin(ang)


def _rope_tables(positions):
    half = QK_ROPE_DIM // 2
    t = positions.size
    inv_freq = ROPE_THETA ** (-jnp.arange(half, dtype=_F32) / half)
    return pl.pallas_call(
        _rope_kernel,
        out_shape=(jax.ShapeDtypeStruct((half, t), _F32),) * 2,
        name="rope_tables",
    )(positions.reshape(1, t).astype(_F32), inv_freq.reshape(half, 1))


def _stage_a_kernel(x_ref, cos_t_ref, sin_t_ref,
                    gmix_ref, win_ref, gq_ref, wq_ref, wqrot_ref,
                    gkv_ref, wk_ref, wv_ref, mu_ref, w0_ref, wwa_ref, a0_ref, g2_ref,
                    kk_ref, ka_ref,
                    q_out, k_out, v_out, r_out, ld_out, kmod_out, vr_out, kkn_out, b_out, g_out,
                    carry_ref):
    tm = x_ref.shape[1]

    @pl.when(pl.program_id(1) == 0)
    def _():
        carry_ref[...] = jnp.zeros_like(carry_ref)

    h = _rms(x_ref[0], gmix_ref[...]).astype(_BF16)
    z = _dot(h, win_ref[...])

    cos2 = jnp.concatenate([cos_t_ref[...]] * 2, axis=0)
    sin2 = jnp.concatenate([sin_t_ref[...]] * 2, axis=0)
    cq = _rms(z[:, A_CQ:A_CQ + Q_LORA_RANK], gq_ref[...]).astype(_BF16)
    q_all = _dot_nt(wq_ref[...], cq)
    q_rot = _dot_nt(wqrot_ref[...], cq)
    q_scale = QK_HEAD_DIM ** -0.5 * LOG2_E
    pad = jnp.zeros((HEAD_SLAB - QK_HEAD_DIM, tm), _BF16)
    rows = []
    for hd in range(MLA_HEADS):
        base = hd * QK_HEAD_DIM
        nope = q_all[base:base + QK_NOPE_DIM]
        rope = (q_all[base + QK_NOPE_DIM:base + QK_HEAD_DIM] * cos2
                + q_rot[hd * QK_ROPE_DIM:(hd + 1) * QK_ROPE_DIM] * sin2)
        rows += [(nope * q_scale).astype(_BF16), (rope * q_scale).astype(_BF16), pad]
    q_out[0] = jnp.concatenate(rows, axis=0)
    ckv = _rms(z[:, A_CKV:A_CKV + KV_LORA_RANK], gkv_ref[...]).astype(_BF16)
    kpe_t = z[:, A_KPE:A_KPE + HEAD_SLAB].T
    kpe_t = kpe_t[:QK_ROPE_DIM] * cos2 + kpe_t[QK_ROPE_DIM:2 * QK_ROPE_DIM] * sin2
    kpe = jnp.concatenate([jnp.zeros((QK_NOPE_DIM, tm), _F32), kpe_t,
                           jnp.zeros((HEAD_SLAB - QK_HEAD_DIM, tm), _F32)], axis=0).T
    k = _dot(ckv, wk_ref[...]) + jnp.concatenate([kpe] * MLA_HEADS, axis=1)
    k_out[0] = k.astype(_BF16)
    v_out[0] = _dot_nt(wv_ref[...], ckv).astype(_BF16)

    zr = z[:, A_RWKV:A_RWKV + RWKV_COLS]
    row = lax.broadcasted_iota(jnp.int32, zr.shape, 0)
    prev = jnp.where(row == 0, carry_ref[...], pltpu.roll(zr, shift=1, axis=0))
    carry_ref[...] = zr[tm - 1:tm, :]
    zs = zr + (prev - zr) * mu_ref[...]
    r = zs[:, 0:RWKV_DIM]
    k_raw = zs[:, RWKV_DIM:2 * RWKV_DIM]
    v_r = zs[:, 2 * RWKV_DIM:3 * RWKV_DIM]
    lo = zs[:, 3 * RWKV_DIM:3 * RWKV_DIM + LANES]
    g_lo = zs[:, 3 * RWKV_DIM + LANES:]
    lane = lax.broadcasted_iota(jnp.int32, lo.shape, 1)
    lo = jnp.where(lane < DECAY_LORA, jnp.tanh(lo), lo).astype(_BF16)
    wa = _dot(lo, wwa_ref[...])
    w_pre = w0_ref[...] + wa[:, :RWKV_DIM]
    neg = -w_pre
    softplus = jnp.maximum(neg, 0.0) + jnp.log(1.0 + jnp.exp(-jnp.abs(neg)))
    log_decay = -jnp.exp(-softplus - 0.5)
    a = jax.nn.sigmoid(a0_ref[...] + wa[:, RWKV_DIM:])
    g = _dot(jax.nn.sigmoid(g_lo).astype(_BF16), g2_ref[...])
    kk = k_raw * kk_ref[...]
    kkn = kk / jnp.maximum(jnp.sqrt(_head_sums(kk * kk)), 1e-12)
    r_out[0] = r
    ld_out[0] = log_decay
    kmod_out[0] = k_raw * (1.0 + (a - 1.0) * ka_ref[...])
    vr_out[0] = v_r
    kkn_out[0] = kkn
    b_out[0] = kkn * a
    g_out[0] = g


def _stage_a(x, cos_t, sin_t, gmix, win, gq, wq, wqrot, gkv, wk, wv, mu, w0, wwa, a0,
             g2, kk, ka):
    b, s, _ = x.shape
    tm = min(TM_A, s)
    tok = lambda bi, si: (bi, si, 0)
    tok_t = lambda bi, si: (bi, 0, si)
    flat_t = lambda bi, si: (0, bi * (s // tm) + si)
    rwkv_tok = jax.ShapeDtypeStruct((b, s, RWKV_DIM), _F32)
    rwkv_spec = pl.BlockSpec((1, tm, RWKV_DIM), tok)
    weights = (gmix, win, gq, wq, wqrot, gkv, wk, wv, mu, w0, wwa, a0, g2, kk, ka)
    return pl.pallas_call(
        _stage_a_kernel,
        grid=(b, s // tm),
        in_specs=[pl.BlockSpec((1, tm, D_MODEL), tok),
                  pl.BlockSpec((cos_t.shape[0], tm), flat_t),
                  pl.BlockSpec((sin_t.shape[0], tm), flat_t)]
                 + [_resident(w.shape) for w in weights],
        out_specs=[pl.BlockSpec((1, MLA_HEADS * HEAD_SLAB, tm), tok_t),
                   pl.BlockSpec((1, tm, MLA_HEADS * HEAD_SLAB), tok),
                   pl.BlockSpec((1, MLA_HEADS * V_HEAD_DIM, tm), tok_t)]
                  + [rwkv_spec] * 7,
        out_shape=[jax.ShapeDtypeStruct((b, MLA_HEADS * HEAD_SLAB, s), _BF16),
                   jax.ShapeDtypeStruct((b, s, MLA_HEADS * HEAD_SLAB), _BF16),
                   jax.ShapeDtypeStruct((b, MLA_HEADS * V_HEAD_DIM, s), _BF16)] + [rwkv_tok] * 7,
        scratch_shapes=[pltpu.VMEM((1, RWKV_COLS), _F32)],
        compiler_params=pltpu.CompilerParams(
            dimension_semantics=("arbitrary", "arbitrary"), vmem_limit_bytes=VMEM_LIMIT),
        name="stage_a",
    )(x, cos_t, sin_t, *weights)


def _attn_kernel(q_ref, k_ref, vt_ref, o_ref, s_a, s_b, mx_a, mx_b, m_ref, acc_ref):
    tq = q_ref.shape[2]
    tk = s_a.shape[1]
    nh = q_ref.shape[1] // HEAD_SLAB
    qi = pl.program_id(2)
    first = 2 * qi
    ones = jnp.ones((ATTN_SUM_ROWS, tk), _BF16)

    m_ref[...] = jnp.full(m_ref.shape, -1e30, _F32)
    acc_ref[...] = jnp.zeros(acc_ref.shape, _F32)

    def scores(kt, s_buf, mx_buf, diag=None):
        off = pl.multiple_of(kt * tk, tk)
        if diag is not None:
            key = lax.broadcasted_iota(jnp.int32, (tk, tq), 0) + diag * tk
            qry = lax.broadcasted_iota(jnp.int32, (tk, tq), 1)
            causal = key <= qry
        for hh in range(nh):
            s = _dot(k_ref[0, pl.ds(off, tk), hh * HEAD_SLAB:(hh + 1) * HEAD_SLAB],
                     q_ref[0, hh * HEAD_SLAB:(hh + 1) * HEAD_SLAB, :])
            if diag is not None:
                s = jnp.where(causal, s, -1e30)
            s_buf[hh] = s
            mx_buf[hh] = jnp.max(s, axis=0, keepdims=True)

    def consume(kt, s_buf, mx_buf):
        off = pl.multiple_of(kt * tk, tk)
        for hh in range(nh):
            m = m_ref[hh]
            m_new = jnp.maximum(m, mx_buf[hh])
            p = jnp.exp2(s_buf[hh] - m_new).astype(_BF16)
            vt = vt_ref[0, hh * V_HEAD_DIM:(hh + 1) * V_HEAD_DIM, pl.ds(off, tk)]
            acc_ref[hh] = (jnp.exp2(m - m_new) * acc_ref[hh]
                           + _dot(jnp.concatenate([vt, ones], axis=0), p))
            m_ref[hh] = m_new

    scores(first, s_a, mx_a, diag=0)
    scores(first + 1, s_b, mx_b, diag=1)
    consume(first, s_a, mx_a)

    def two_steps(i2, _):
        i = 2 * i2
        scores(i, s_a, mx_a)
        consume(jnp.where(i2 == 0, first + 1, i - 1), s_b, mx_b)
        scores(i + 1, s_b, mx_b)
        consume(i, s_a, mx_a)
        return 0

    lax.fori_loop(0, qi, two_steps, 0)
    consume(jnp.where(qi == 0, first + 1, first - 1), s_b, mx_b)

    outs = [acc_ref[hh, :V_HEAD_DIM] / acc_ref[hh, V_HEAD_DIM:V_HEAD_DIM + 1] for hh in range(nh)]
    o_ref[0] = jnp.concatenate(outs, axis=0).T.astype(_BF16)


def _attention(qt, k, vt):
    b, s, _ = k.shape
    tq = min(TQ, s)
    groups = MLA_HEADS // ATTN_HEADS
    qk_w = ATTN_HEADS * HEAD_SLAB
    v_w = ATTN_HEADS * V_HEAD_DIM
    return pl.pallas_call(
        _attn_kernel,
        grid=(b, groups, s // tq),
        in_specs=[pl.BlockSpec((1, qk_w, tq), lambda bi, hg, qi: (bi, hg, qi)),
                  pl.BlockSpec((1, s, qk_w), lambda bi, hg, qi: (bi, 0, hg),
                               pipeline_mode=pl.Buffered(1)),
                  pl.BlockSpec((1, v_w, s), lambda bi, hg, qi: (bi, hg, 0),
                               pipeline_mode=pl.Buffered(1))],
        out_specs=pl.BlockSpec((1, tq, v_w), lambda bi, hg, qi: (bi, qi, hg)),
        out_shape=jax.ShapeDtypeStruct((b, s, MLA_HEADS * V_HEAD_DIM), _BF16),
        scratch_shapes=[pltpu.VMEM((ATTN_HEADS, tq // 2, tq), _F32),
                        pltpu.VMEM((ATTN_HEADS, tq // 2, tq), _F32),
                        pltpu.VMEM((ATTN_HEADS, 1, tq), _F32),
                        pltpu.VMEM((ATTN_HEADS, 1, tq), _F32),
                        pltpu.VMEM((ATTN_HEADS, 1, tq), _F32),
                        pltpu.VMEM((ATTN_HEADS, V_HEAD_DIM + ATTN_SUM_ROWS, tq), _F32)],
        compiler_params=pltpu.CompilerParams(
            dimension_semantics=("parallel", "parallel", "arbitrary"),
            vmem_limit_bytes=VMEM_LIMIT),
        name="mla_attention",
    )(qt, k, vt)


def _bdot(a, b):
    return jnp.einsum("nij,njk->nik", a, b, preferred_element_type=_F32)


def _bdot_nt(a, b):
    return jnp.einsum("nik,njk->nij", a, b, preferred_element_type=_F32)


def _bdot_tn(a, b):
    return jnp.einsum("nki,nkj->nij", a, b, preferred_element_type=_F32)


def _wkv_chunk_terms(r, ld, k, v, kk, b):
    n, c, gw = r.shape
    heads = gw // RWKV_HEAD_DIM
    row = lax.broadcasted_iota(jnp.int32, (c, gw), 0)
    col = lax.broadcasted_iota(jnp.int32, (c, gw), 1) % c
    strict = (col < row)[None]
    incl = (col <= row)[None]
    diff = (row ^ col)[None]
    same_head = _same_head((gw, gw))[None]

    def blockdiag(x):
        return jnp.where(same_head, jnp.concatenate([x] * heads, axis=1), jnp.zeros((), x.dtype))

    trow = lax.broadcasted_iota(jnp.int32, (c, c), 0)
    tcol = lax.broadcasted_iota(jnp.int32, (c, c), 1)
    tri = jnp.broadcast_to((tcol <= trow).astype(_BF16)[None], (n, c, c))
    ld_hi = ld.astype(_BF16)
    ld_lo = (ld - ld_hi.astype(_F32)).astype(_BF16)
    cum = _bdot(tri, ld_hi) + _bdot(tri, ld_lo)
    cum_last = cum[:, c - 1:c, :]
    e_neg = jnp.exp(-cum)
    e_tail = jnp.exp(cum_last - cum)
    r_hat = r * jnp.exp(cum)
    a_hat_bf = (-kk * jnp.exp(cum - ld)).astype(_BF16)
    lhs = jnp.concatenate([a_hat_bf, r_hat.astype(_BF16)], axis=1)
    v_bf = v.astype(_BF16)
    a_b = _bdot_nt(lhs, blockdiag((b * e_neg).astype(_BF16)))
    a_k = _bdot_nt(lhs, blockdiag((k * e_neg).astype(_BF16)))
    a_ab = jnp.where(strict, a_b[:, :c], 0.0)
    a_rb = jnp.where(incl, a_b[:, c:], 0.0).astype(_BF16)
    a_akrk = jnp.concatenate([jnp.where(strict, a_k[:, :c], 0.0),
                              jnp.where(incl, a_k[:, c:], 0.0)], axis=1).astype(_BF16)

    t = jnp.where(diff == 0, 1.0, 0.0) + jnp.where(diff == 1, a_ab, 0.0)
    size = 2
    while size < c:
        a_l = jnp.where((diff >= size) & (diff < 2 * size), a_ab, 0.0).astype(_BF16)
        t_bf = t.astype(_BF16)
        t = t + _bdot(_bdot(t_bf, blockdiag(a_l)).astype(_BF16), blockdiag(t_bf))
        size *= 2
    t_bf = t.astype(_BF16)

    v_part = _bdot(a_akrk, blockdiag(v_bf))
    w = _bdot(t_bf, blockdiag(a_hat_bf)).astype(_BF16)
    u0 = _bdot(t_bf, blockdiag(v_part[:, :c].astype(_BF16))).astype(_BF16)
    qe = r_hat + _bdot(a_rb, blockdiag(w))
    y0 = v_part[:, c:] + _bdot(a_rb, blockdiag(u0))
    b_bar = (b * e_tail).astype(_BF16)
    g = jnp.where(same_head, _bdot_tn(w, b_bar), 0.0)
    h = jnp.where(same_head,
                  _bdot_tn(jnp.concatenate([v_bf, u0], axis=1),
                           jnp.concatenate([(k * e_tail).astype(_BF16), b_bar], axis=1)), 0.0)
    return qe.astype(_BF16), y0, g.astype(_BF16), h, jnp.exp(cum_last)


def _wkv_kernel(r_ref, ld_ref, k_ref, v_ref, kk_ref, b_ref, g_ref, rk_ref, lnw_ref, lnb_ref,
                o_ref, state_ref):
    blk, dim = r_ref.shape[1:]
    c = WKV_CHUNK
    nc = blk // c
    groups = dim // WKV_GROUP

    @pl.when(pl.program_id(1) == 0)
    def _():
        state_ref[...] = jnp.zeros_like(state_ref)

    def problems(ref):
        x = ref[0].reshape(nc, c, dim)
        return jnp.concatenate([x[:, :, g0:g0 + WKV_GROUP] for g0 in range(0, dim, WKV_GROUP)],
                               axis=0)

    terms = _wkv_chunk_terms(problems(r_ref), problems(ld_ref), problems(k_ref), problems(v_ref),
                             problems(kk_ref), problems(b_ref))
    qe, y0, g, h, decay = (x.reshape((groups, nc) + x.shape[1:]) for x in terms)

    state = state_ref[...]
    ys = []
    for ci in range(nc):
        state_bf = state.astype(_BF16)
        ys.append(_bdot_nt(qe[:, ci], state_bf) + y0[:, ci])
        state = state * decay[:, ci] + _bdot(state_bf, g[:, ci]) + h[:, ci]
    state_ref[...] = state
    y = jnp.concatenate(ys, axis=1)
    y = jnp.concatenate([y[gi] for gi in range(groups)], axis=-1)

    inv_n = 1.0 / RWKV_HEAD_DIM
    centered = y - _head_sums(y) * inv_n
    var = _head_sums(centered * centered) * inv_n
    yn = centered * lax.rsqrt(var + GN_EPS) * lnw_ref[...] + lnb_ref[...]
    bonus = _head_sums(r_ref[0] * k_ref[0] * rk_ref[...]) * v_ref[0]
    o_ref[0] = ((yn + bonus) * g_ref[0]).astype(_BF16)


def _wkv(r, ld, k, v, kk, bvec, g, rk, lnw, lnb):
    b, s, dim = r.shape
    blk = min(WKV_BLOCK, s)
    tok_spec = pl.BlockSpec((1, blk, dim), lambda bi, si: (bi, si, 0))
    return pl.pallas_call(
        _wkv_kernel,
        grid=(b, s // blk),
        in_specs=[tok_spec] * 7 + [_resident(rk.shape), _resident(lnw.shape), _resident(lnb.shape)],
        out_specs=tok_spec,
        out_shape=jax.ShapeDtypeStruct((b, s, dim), _BF16),
        scratch_shapes=[pltpu.VMEM((dim // WKV_GROUP, WKV_GROUP, WKV_GROUP), _F32)],
        compiler_params=pltpu.CompilerParams(
            dimension_semantics=("parallel", "arbitrary"), vmem_limit_bytes=VMEM_LIMIT),
        name="wkv7",
    )(r, ld, k, v, kk, bvec, g, rk, lnw, lnb)


def _stage_d_kernel(x_ref, oa_ref, ob_ref, p_ref, gmix_ref, wgate_ref, woa_ref, wob_ref, wout_ref,
                    gffn_ref, wup_ref, wdown_ref, gple_ref, wpg_ref, wpp_ref, gfin_ref, out_ref, *,
                    final_norm):
    x = x_ref[...]
    h = _rms(x, gmix_ref[...]).astype(_BF16)
    gate = jax.nn.sigmoid(_dot(h, wgate_ref[...]))
    y_a = _dot(oa_ref[...], woa_ref[...])
    y_b = _dot(ob_ref[...], wob_ref[...])
    mix = gate[:, :D_MODEL] * y_a + gate[:, D_MODEL:] * y_b
    x = x + _dot(mix.astype(_BF16), wout_ref[...])
    h = _rms(x, gffn_ref[...]).astype(_BF16)
    for c0 in range(0, D_FF, FF_CHUNK):
        hid = jnp.square(jnp.maximum(_dot(h, wup_ref[:, c0:c0 + FF_CHUNK]), 0.0))
        x = x + _dot(hid.astype(_BF16), wdown_ref[c0:c0 + FF_CHUNK, :])
    ple_gate = jax.nn.sigmoid(_dot(_rms(x, gple_ref[...]).astype(_BF16), wpg_ref[...]))
    x = x + ple_gate * _dot(p_ref[...].astype(_BF16), wpp_ref[...])
    out_ref[...] = _rms(x, gfin_ref[...]) if final_norm else x


def _stage_d(x, oa, ob, p, gmix, wgate, woa, wob, wout, gffn, wup, wdown, gple, wpg, wpp, gfin,
             final_norm):
    t = x.shape[0]
    tm = min(TM_D, t)
    tok = lambda i: (i, 0)
    weights = (gmix, wgate, woa, wob, wout, gffn, wup, wdown, gple, wpg, wpp, gfin)
    return pl.pallas_call(
        functools.partial(_stage_d_kernel, final_norm=final_norm),
        grid=(t // tm,),
        in_specs=[pl.BlockSpec((tm, D_MODEL), tok),
                  pl.BlockSpec((tm, oa.shape[1]), tok),
                  pl.BlockSpec((tm, ob.shape[1]), tok),
                  pl.BlockSpec((tm, PLE_DIM), tok)] + [_resident(w.shape) for w in weights],
        out_specs=pl.BlockSpec((tm, D_MODEL), tok),
        out_shape=jax.ShapeDtypeStruct((t, D_MODEL), _F32),
        compiler_params=pltpu.CompilerParams(
            dimension_semantics=("parallel",), vmem_limit_bytes=VMEM_LIMIT),
        name="stage_d",
    )(x, oa, ob, p, *weights)


def _rotate_half(w_pe):
    half = w_pe.shape[-1] // 2
    return jnp.concatenate([-w_pe[..., half:], w_pe[..., :half]], axis=-1)


def _prepare_layer(i, g_mix, w_in, g_q_a, w_uq, g_kv_a, w_ukv, w_o_mla, mu_rwkv, w0, w2, a0, a2,
                   g2, k_k, k_a, r_k, ln_x_w, ln_x_b, w_o_rwkv, w_out, g_ffn, w_ffn_up, w_ffn_down,
                   g_ple, w_ple_gate, w_ple_proj):
    row = lambda a: a[i].reshape(1, -1)
    w = w_in[i].astype(_BF16)
    kpe = w[:, Q_LORA_RANK + KV_LORA_RANK:MLA_COLS]
    kpe_slab = jnp.pad(jnp.concatenate([kpe, _rotate_half(kpe)], axis=1),
                       ((0, 0), (0, HEAD_SLAB - 2 * QK_ROPE_DIM)))
    win = jnp.concatenate([w[:, :Q_LORA_RANK + KV_LORA_RANK], kpe_slab,
                           w[:, MLA_COLS:MLA_COLS + RWKV_COLS]], axis=1)
    wgate = w[:, MLA_COLS + RWKV_COLS:]
    uq = w_uq[i].astype(_BF16)
    wq = uq.T
    uq_pe = uq.reshape(-1, MLA_HEADS, QK_HEAD_DIM)[:, :, QK_NOPE_DIM:]
    wqrot = _rotate_half(uq_pe).reshape(-1, MLA_HEADS * QK_ROPE_DIM).T
    ukv = w_ukv[i].astype(_BF16).reshape(-1, MLA_HEADS, QK_NOPE_DIM + V_HEAD_DIM)
    wk = jnp.pad(ukv[:, :, :QK_NOPE_DIM], ((0, 0), (0, 0), (0, HEAD_SLAB - QK_NOPE_DIM))).reshape(
        -1, MLA_HEADS * HEAD_SLAB)
    wv = ukv[:, :, QK_NOPE_DIM:].reshape(-1, MLA_HEADS * V_HEAD_DIM).T
    wwa = jnp.concatenate(
        [jnp.concatenate([w2[i], jnp.zeros_like(w2[i])], axis=1),
         jnp.concatenate([jnp.zeros_like(a2[i]), a2[i]], axis=1)], axis=0).astype(_BF16)
    stage_a = (row(g_mix), win, row(g_q_a), wq, wqrot, row(g_kv_a), wk, wv, row(mu_rwkv), row(w0),
               wwa, row(a0), g2[i].astype(_BF16), row(k_k), row(k_a))
    wkv = (r_k[i].reshape(1, -1), row(ln_x_w), row(ln_x_b))
    stage_d = (row(g_mix), wgate, w_o_mla[i].astype(_BF16), w_o_rwkv[i].astype(_BF16),
               w_out[i].astype(_BF16), row(g_ffn), w_ffn_up[i].astype(_BF16),
               w_ffn_down[i].astype(_BF16), row(g_ple), w_ple_gate[i].astype(_BF16),
               w_ple_proj[i].astype(_BF16))
    return stage_a, wkv, stage_d


def kernel(x, p, positions, g_mix, w_in, g_q_a, w_uq, g_kv_a, w_ukv, w_o_mla, mu_rwkv, w0, w2, a0, a2, g2, k_k, k_a, r_k, ln_x_w, ln_x_b, w_o_rwkv, w_out, g_ffn, w_ffn_up, w_ffn_down, g_ple, w_ple_gate, w_ple_proj, g_final):
    b, s, d = x.shape
    depth = w_in.shape[0]
    t = b * s
    cos_t, sin_t = _rope_tables(positions)
    gfin = g_final.reshape(1, -1)
    for i in range(depth):
        sa, sw, sd = _prepare_layer(i, g_mix, w_in, g_q_a, w_uq, g_kv_a, w_ukv, w_o_mla, mu_rwkv,
                                    w0, w2, a0, a2, g2, k_k, k_a, r_k, ln_x_w, ln_x_b, w_o_rwkv,
                                    w_out, g_ffn, w_ffn_up, w_ffn_down, g_ple, w_ple_gate,
                                    w_ple_proj)
        q, k, v, r, ld, kmod, vr, kkn, bvec, g = _stage_a(x, cos_t, sin_t, *sa)
        o_a = _attention(q, k, v)
        o_b = _wkv(r, ld, kmod, vr, kkn, bvec, g, *sw)
        out = _stage_d(x.reshape(t, d), o_a.reshape(t, -1), o_b.reshape(t, -1), p[i].reshape(t, -1),
                       *sd, gfin, final_norm=(i == depth - 1))
        x = out.reshape(b, s, d)
    return x
```

```python
import functools

import jax
import jax.numpy as jnp
from jax import lax
from jax.experimental import pallas as pl
from jax.experimental.pallas import tpu as pltpu

D_MODEL = 1024
MLA_HEADS = 8
QK_NOPE_DIM = 64
QK_ROPE_DIM = 32
QK_HEAD_DIM = QK_NOPE_DIM + QK_ROPE_DIM
V_HEAD_DIM = 64
Q_LORA_RANK = 384
KV_LORA_RANK = 256
ROPE_THETA = 10000.0
RWKV_HEADS = 8
RWKV_HEAD_DIM = 64
RWKV_DIM = RWKV_HEADS * RWKV_HEAD_DIM
DECAY_LORA = 64
AAA_LORA = 64
GATE_LORA = 128
GN_EPS = RWKV_HEAD_DIM * 1e-5
MLA_COLS = Q_LORA_RANK + KV_LORA_RANK + QK_ROPE_DIM
RWKV_COLS = 3 * RWKV_DIM + DECAY_LORA + AAA_LORA + GATE_LORA
D_FF = 4 * D_MODEL
PLE_DIM = 256
RMS_EPS = 1e-6
LOG2_E = 1.4426950408889634
MASK_VALUE = -1e30

LANES = 128
HEAD_SLAB = LANES
VMEM_LIMIT = 56 * 1024 * 1024

TM_A = 512
TQ = 512
ATTN_HEADS = 8
ATTN_SUM_ROWS = 16
WKV_CHUNK = 64
WKV_BLOCK = 512
WKV_GROUP = 256
TM_D = 512
FF_CHUNK = 1024

A_CQ = 0
A_CKV = A_CQ + Q_LORA_RANK
A_KPE = A_CKV + KV_LORA_RANK
A_RWKV = A_KPE + HEAD_SLAB
A_COLS = A_RWKV + RWKV_COLS

_BF16 = jnp.bfloat16
_F32 = jnp.float32


def _dot(a, b):
    return jnp.dot(a, b, preferred_element_type=_F32)


def _dot_nt(a, b):
    return lax.dot_general(a, b, (((1,), (1,)), ((), ())), preferred_element_type=_F32)


def _dot_tn(a, b):
    return lax.dot_general(a, b, (((0,), (0,)), ((), ())), preferred_element_type=_F32)


def _rms(x, g):
    return x * lax.rsqrt(jnp.mean(x * x, axis=-1, keepdims=True) + RMS_EPS) * g


def _same_head(shape):
    row = lax.broadcasted_iota(jnp.int32, shape, len(shape) - 2) // RWKV_HEAD_DIM
    col = lax.broadcasted_iota(jnp.int32, shape, len(shape) - 1) // RWKV_HEAD_DIM
    return row == col


def _head_sums(x):
    ones_bd = jnp.where(_same_head((WKV_GROUP, WKV_GROUP)), 1.0, 0.0).astype(_BF16)
    xb = x.astype(_BF16)
    return jnp.concatenate([_dot(xb[:, g0:g0 + WKV_GROUP], ones_bd)
                            for g0 in range(0, x.shape[1], WKV_GROUP)], axis=1)


def _resident(shape):
    zeros = (0,) * len(shape)
    return pl.BlockSpec(shape, lambda *_: zeros, pipeline_mode=pl.Buffered(1))


def _rope_kernel(pos_ref, inv_ref, cos_ref, sin_ref):
    ang = pos_ref[...] * inv_ref[...]
    cos_ref[...] = jnp.cos(ang)
    sin_ref[...] = jnp.sin(ang)


def _rope_tables(positions):
    half = QK_ROPE_DIM // 2
    t = positions.size
    inv_freq = ROPE_THETA ** (-jnp.arange(half, dtype=_F32) / half)
    return pl.pallas_call(
        _rope_kernel,
        out_shape=(jax.ShapeDtypeStruct((half, t), _F32),) * 2,
        name="rope_tables",
    )(positions.reshape(1, t).astype(_F32), inv_freq.reshape(half, 1))


def _stage_a_kernel(x_ref, cos_t_ref, sin_t_ref,
                    gmix_ref, win_ref, gq_ref, wq_ref, wqrot_ref,
                    gkv_ref, wk_ref, wv_ref, mu_ref, w0_ref, wwa_ref, a0_ref, g2_ref,
                    kk_ref, ka_ref,
                    q_out, k_out, v_out, r_out, ld_out, kmod_out, vr_out, kkn_out, b_out, g_out,
                    carry_ref):
    tm = x_ref.shape[1]

    @pl.when(pl.program_id(1) == 0)
    def _():
        carry_ref[...] = jnp.zeros_like(carry_ref)

    h = _rms(x_ref[0], gmix_ref[...]).astype(_BF16)
    z = _dot(h, win_ref[...])

    cos2 = jnp.concatenate([cos_t_ref[...]] * 2, axis=0)
    sin2 = jnp.concatenate([sin_t_ref[...]] * 2, axis=0)
    cq = _rms(z[:, A_CQ:A_CQ + Q_LORA_RANK], gq_ref[...]).astype(_BF16)
    q_all = _dot_nt(wq_ref[...], cq)
    q_rot = _dot_nt(wqrot_ref[...], cq)
    q_scale = QK_HEAD_DIM ** -0.5 * LOG2_E
    pad = jnp.zeros((HEAD_SLAB - QK_HEAD_DIM, tm), _BF16)
    rows = []
    for hd in range(MLA_HEADS):
        base = hd * QK_HEAD_DIM
        nope = q_all[base:base + QK_NOPE_DIM]
        rope = (q_all[base + QK_NOPE_DIM:base + QK_HEAD_DIM] * cos2
                + q_rot[hd * QK_ROPE_DIM:(hd + 1) * QK_ROPE_DIM] * sin2)
        rows += [(nope * q_scale).astype(_BF16), (rope * q_scale).astype(_BF16), pad]
    q_out[0] = jnp.concatenate(rows, axis=0)
    ckv = _rms(z[:, A_CKV:A_CKV + KV_LORA_RANK], gkv_ref[...]).astype(_BF16)
    kpe_t = z[:, A_KPE:A_KPE + HEAD_SLAB].T
    kpe_t = kpe_t[:QK_ROPE_DIM] * cos2 + kpe_t[QK_ROPE_DIM:2 * QK_ROPE_DIM] * sin2
    kpe = jnp.concatenate([jnp.zeros((QK_NOPE_DIM, tm), _F32), kpe_t,
                           jnp.zeros((HEAD_SLAB - QK_HEAD_DIM, tm), _F32)], axis=0).T
    k = _dot(ckv, wk_ref[...]) + jnp.concatenate([kpe] * MLA_HEADS, axis=1)
    k_out[0] = k.astype(_BF16)
    v_out[0] = _dot_nt(wv_ref[...], ckv).astype(_BF16)

    zr = z[:, A_RWKV:A_RWKV + RWKV_COLS]
    row = lax.broadcasted_iota(jnp.int32, zr.shape, 0)
    prev = jnp.where(row == 0, carry_ref[...], pltpu.roll(zr, shift=1, axis=0))
    carry_ref[...] = zr[tm - 1:tm, :]
    zs = zr + (prev - zr) * mu_ref[...]
    r = zs[:, 0:RWKV_DIM]
    k_raw = zs[:, RWKV_DIM:2 * RWKV_DIM]
    v_r = zs[:, 2 * RWKV_DIM:3 * RWKV_DIM]
    lo = zs[:, 3 * RWKV_DIM:3 * RWKV_DIM + LANES]
    g_lo = zs[:, 3 * RWKV_DIM + LANES:]
    lane = lax.broadcasted_iota(jnp.int32, lo.shape, 1)
    lo = jnp.where(lane < DECAY_LORA, jnp.tanh(lo), lo).astype(_BF16)
    wa = _dot(lo, wwa_ref[...])
    w_pre = w0_ref[...] + wa[:, :RWKV_DIM]
    neg = -w_pre
    softplus = jnp.maximum(neg, 0.0) + jnp.log(1.0 + jnp.exp(-jnp.abs(neg)))
    log_decay = -jnp.exp(-softplus - 0.5)
    a = jax.nn.sigmoid(a0_ref[...] + wa[:, RWKV_DIM:])
    g = _dot(jax.nn.sigmoid(g_lo).astype(_BF16), g2_ref[...])
    kk = k_raw * kk_ref[...]
    kkn = kk / jnp.maximum(jnp.sqrt(_head_sums(kk * kk)), 1e-12)
    r_out[0] = r
    ld_out[0] = log_decay
    kmod_out[0] = k_raw * (1.0 + (a - 1.0) * ka_ref[...])
    vr_out[0] = v_r
    kkn_out[0] = kkn
    b_out[0] = kkn * a
    g_out[0] = g


def _stage_a(x, cos_t, sin_t, gmix, win, gq, wq, wqrot, gkv, wk, wv, mu, w0, wwa, a0,
             g2, kk, ka):
    b, s, _ = x.shape
    tm = min(TM_A, s)
    tok = lambda bi, si: (bi, si, 0)
    tok_t = lambda bi, si: (bi, 0, si)
    flat_t = lambda bi, si: (0, bi * (s // tm) + si)
    rwkv_tok = jax.ShapeDtypeStruct((b, s, RWKV_DIM), _F32)
    rwkv_spec = pl.BlockSpec((1, tm, RWKV_DIM), tok)
    weights = (gmix, win, gq, wq, wqrot, gkv, wk, wv, mu, w0, wwa, a0, g2, kk, ka)
    return pl.pallas_call(
        _stage_a_kernel,
        grid=(b, s // tm),
        in_specs=[pl.BlockSpec((1, tm, D_MODEL), tok),
                  pl.BlockSpec((cos_t.shape[0], tm), flat_t),
                  pl.BlockSpec((sin_t.shape[0], tm), flat_t)]
                 + [_resident(w.shape) for w in weights],
        out_specs=[pl.BlockSpec((1, MLA_HEADS * HEAD_SLAB, tm), tok_t),
                   pl.BlockSpec((1, tm, MLA_HEADS * HEAD_SLAB), tok),
                   pl.BlockSpec((1, MLA_HEADS * V_HEAD_DIM, tm), tok_t)]
                  + [rwkv_spec] * 7,
        out_shape=[jax.ShapeDtypeStruct((b, MLA_HEADS * HEAD_SLAB, s), _BF16),
                   jax.ShapeDtypeStruct((b, s, MLA_HEADS * HEAD_SLAB), _BF16),
                   jax.ShapeDtypeStruct((b, MLA_HEADS * V_HEAD_DIM, s), _BF16)] + [rwkv_tok] * 7,
        scratch_shapes=[pltpu.VMEM((1, RWKV_COLS), _F32)],
        compiler_params=pltpu.CompilerParams(
            dimension_semantics=("arbitrary", "arbitrary"), vmem_limit_bytes=VMEM_LIMIT),
        name="stage_a",
    )(x, cos_t, sin_t, *weights)


def _attn_kernel(q_ref, k_ref, vt_ref, o_ref, s_a, s_b, mx_a, mx_b, m_ref, acc_ref):
    tq = q_ref.shape[2]
    nh = q_ref.shape[1] // HEAD_SLAB
    qi = pl.program_id(2)
    key = lax.broadcasted_iota(jnp.int32, (tq, tq), 0)
    qry = lax.broadcasted_iota(jnp.int32, (tq, tq), 1)
    causal = key <= qry
    ones = jnp.ones((ATTN_SUM_ROWS, tq), _BF16)

    m_ref[...] = jnp.full(m_ref.shape, MASK_VALUE, _F32)
    acc_ref[...] = jnp.zeros(acc_ref.shape, _F32)

    def scores(kt, s_buf, mx_buf, masked=False):
        off = pl.multiple_of(kt * tq, tq)
        for hh in range(nh):
            s = _dot(k_ref[0, pl.ds(off, tq), hh * HEAD_SLAB:(hh + 1) * HEAD_SLAB],
                     q_ref[0, hh * HEAD_SLAB:(hh + 1) * HEAD_SLAB, :])
            if masked:
                s = jnp.where(causal, s, MASK_VALUE)
            s_buf[hh] = s
            mx_buf[hh] = jnp.max(s, axis=0, keepdims=True)

    def consume(kt, s_buf, mx_buf):
        off = pl.multiple_of(kt * tq, tq)
        for hh in range(nh):
            m = m_ref[hh]
            m_new = jnp.maximum(m, mx_buf[hh])
            p = jnp.exp2(s_buf[hh] - m_new).astype(_BF16)
            vt = vt_ref[0, hh * V_HEAD_DIM:(hh + 1) * V_HEAD_DIM, pl.ds(off, tq)]
            acc_ref[hh] = (jnp.exp2(m - m_new) * acc_ref[hh]
                           + _dot(jnp.concatenate([vt, ones], axis=0), p))
            m_ref[hh] = m_new

    def tile_at(i):
        return jnp.where(i == 0, qi, i - 1)

    scores(qi, s_a, mx_a, masked=True)

    def two_steps(i2, _):
        i = 2 * i2
        scores(i, s_b, mx_b)
        consume(tile_at(i), s_a, mx_a)
        scores(i + 1, s_a, mx_a)
        consume(i, s_b, mx_b)
        return 0

    lax.fori_loop(0, qi // 2, two_steps, 0)

    @pl.when(qi % 2 == 1)
    def _():
        scores(qi - 1, s_b, mx_b)
        consume(tile_at(qi - 1), s_a, mx_a)
        consume(qi - 1, s_b, mx_b)

    @pl.when(qi % 2 == 0)
    def _():
        consume(tile_at(qi), s_a, mx_a)

    outs = [acc_ref[hh, :V_HEAD_DIM] / acc_ref[hh, V_HEAD_DIM:V_HEAD_DIM + 1] for hh in range(nh)]
    o_ref[0] = jnp.concatenate(outs, axis=0).T.astype(_BF16)


def _attention(qt, k, vt):
    b, s, _ = k.shape
    tq = min(TQ, s)
    groups = MLA_HEADS // ATTN_HEADS
    qk_w = ATTN_HEADS * HEAD_SLAB
    v_w = ATTN_HEADS * V_HEAD_DIM
    return pl.pallas_call(
        _attn_kernel,
        grid=(b, groups, s // tq),
        in_specs=[pl.BlockSpec((1, qk_w, tq), lambda bi, hg, qi: (bi, hg, qi)),
                  pl.BlockSpec((1, s, qk_w), lambda bi, hg, qi: (bi, 0, hg),
                               pipeline_mode=pl.Buffered(1)),
                  pl.BlockSpec((1, v_w, s), lambda bi, hg, qi: (bi, hg, 0),
                               pipeline_mode=pl.Buffered(1))],
        out_specs=pl.BlockSpec((1, tq, v_w), lambda bi, hg, qi: (bi, qi, hg)),
        out_shape=jax.ShapeDtypeStruct((b, s, MLA_HEADS * V_HEAD_DIM), _BF16),
        scratch_shapes=[pltpu.VMEM((ATTN_HEADS, tq, tq), _F32),
                        pltpu.VMEM((ATTN_HEADS, tq, tq), _F32),
                        pltpu.VMEM((ATTN_HEADS, 1, tq), _F32),
                        pltpu.VMEM((ATTN_HEADS, 1, tq), _F32),
                        pltpu.VMEM((ATTN_HEADS, 1, tq), _F32),
                        pltpu.VMEM((ATTN_HEADS, V_HEAD_DIM + ATTN_SUM_ROWS, tq), _F32)],
        compiler_params=pltpu.CompilerParams(
            dimension_semantics=("parallel", "parallel", "arbitrary"),
            vmem_limit_bytes=VMEM_LIMIT),
        name="mla_attention",
    )(qt, k, vt)


def _bdot(a, b):
    return jnp.einsum("nij,njk->nik", a, b, preferred_element_type=_F32)


def _bdot_nt(a, b):
    return jnp.einsum("nik,njk->nij", a, b, preferred_element_type=_F32)


def _bdot_tn(a, b):
    return jnp.einsum("nki,nkj->nij", a, b, preferred_element_type=_F32)


def _wkv_chunk_terms(r, ld, k, v, kk, b):
    n, c, gw = r.shape
    heads = gw // RWKV_HEAD_DIM
    row = lax.broadcasted_iota(jnp.int32, (c, gw), 0)
    col = lax.broadcasted_iota(jnp.int32, (c, gw), 1) % c
    strict = (col < row)[None]
    incl = (col <= row)[None]
    diff = (row ^ col)[None]
    same_head = _same_head((gw, gw))[None]

    def blockdiag(x):
        return jnp.where(same_head, jnp.concatenate([x] * heads, axis=1), jnp.zeros((), x.dtype))

    trow = lax.broadcasted_iota(jnp.int32, (c, c), 0)
    tcol = lax.broadcasted_iota(jnp.int32, (c, c), 1)
    tri = jnp.broadcast_to((tcol <= trow).astype(_BF16)[None], (n, c, c))
    ld_hi = ld.astype(_BF16)
    ld_lo = (ld - ld_hi.astype(_F32)).astype(_BF16)
    cum = _bdot(tri, ld_hi) + _bdot(tri, ld_lo)
    cum_last = cum[:, c - 1:c, :]
    e_neg = jnp.exp(-cum)
    e_tail = jnp.exp(cum_last - cum)
    r_hat = r * jnp.exp(cum)
    a_hat_bf = (-kk * jnp.exp(cum - ld)).astype(_BF16)
    lhs = jnp.concatenate([a_hat_bf, r_hat.astype(_BF16)], axis=1)
    v_bf = v.astype(_BF16)
    a_b = _bdot_nt(lhs, blockdiag((b * e_neg).astype(_BF16)))
    a_k = _bdot_nt(lhs, blockdiag((k * e_neg).astype(_BF16)))
    a_ab = jnp.where(strict, a_b[:, :c], 0.0)
    a_rb = jnp.where(incl, a_b[:, c:], 0.0).astype(_BF16)
    a_akrk = jnp.concatenate([jnp.where(strict, a_k[:, :c], 0.0),
                              jnp.where(incl, a_k[:, c:], 0.0)], axis=1).astype(_BF16)

    t = jnp.where(diff == 0, 1.0, 0.0) + jnp.where(diff == 1, a_ab, 0.0)
    size = 2
    while size < c:
        a_l = jnp.where((diff >= size) & (diff < 2 * size), a_ab, 0.0).astype(_BF16)
        t_bf = t.astype(_BF16)
        t = t + _bdot(_bdot(t_bf, blockdiag(a_l)).astype(_BF16), blockdiag(t_bf))
        size *= 2
    t_bf = t.astype(_BF16)

    v_part = _bdot(a_akrk, blockdiag(v_bf))
    w = _bdot(t_bf, blockdiag(a_hat_bf)).astype(_BF16)
    u0 = _bdot(t_bf, blockdiag(v_part[:, :c].astype(_BF16))).astype(_BF16)
    qe = r_hat + _bdot(a_rb, blockdiag(w))
    y0 = v_part[:, c:] + _bdot(a_rb, blockdiag(u0))
    b_bar = (b * e_tail).astype(_BF16)
    g = jnp.where(same_head, _bdot_tn(w, b_bar), 0.0)
    h = jnp.where(same_head,
                  _bdot_tn(jnp.concatenate([v_bf, u0], axis=1),
                           jnp.concatenate([(k * e_tail).astype(_BF16), b_bar], axis=1)), 0.0)
    return qe.astype(_BF16), y0, g.astype(_BF16), h, jnp.exp(cum_last)


def _wkv_kernel(r_ref, ld_ref, k_ref, v_ref, kk_ref, b_ref, g_ref, rk_ref, lnw_ref, lnb_ref,
                o_ref, state_ref):
    blk, dim = r_ref.shape[1:]
    c = WKV_CHUNK
    nc = blk // c
    groups = dim // WKV_GROUP

    @pl.when(pl.program_id(1) == 0)
    def _():
        state_ref[...] = jnp.zeros_like(state_ref)

    def problems(ref):
        x = ref[0].reshape(nc, c, dim)
        return jnp.concatenate([x[:, :, g0:g0 + WKV_GROUP] for g0 in range(0, dim, WKV_GROUP)],
                               axis=0)

    terms = _wkv_chunk_terms(problems(r_ref), problems(ld_ref), problems(k_ref), problems(v_ref),
                             problems(kk_ref), problems(b_ref))
    qe, y0, g, h, decay = (x.reshape((groups, nc) + x.shape[1:]) for x in terms)

    state = state_ref[...]
    ys = []
    for ci in range(nc):
        state_bf = state.astype(_BF16)
        ys.append(_bdot_nt(qe[:, ci], state_bf) + y0[:, ci])
        state = state * decay[:, ci] + _bdot(state_bf, g[:, ci]) + h[:, ci]
    state_ref[...] = state
    y = jnp.concatenate(ys, axis=1)
    y = jnp.concatenate([y[gi] for gi in range(groups)], axis=-1)

    inv_n = 1.0 / RWKV_HEAD_DIM
    centered = y - _head_sums(y) * inv_n
    var = _head_sums(centered * centered) * inv_n
    yn = centered * lax.rsqrt(var + GN_EPS) * lnw_ref[...] + lnb_ref[...]
    bonus = _head_sums(r_ref[0] * k_ref[0] * rk_ref[...]) * v_ref[0]
    o_ref[0] = ((yn + bonus) * g_ref[0]).astype(_BF16)


def _wkv(r, ld, k, v, kk, bvec, g, rk, lnw, lnb):
    b, s, dim = r.shape
    blk = min(WKV_BLOCK, s)
    tok_spec = pl.BlockSpec((1, blk, dim), lambda bi, si: (bi, si, 0))
    return pl.pallas_call(
        _wkv_kernel,
        grid=(b, s // blk),
        in_specs=[tok_spec] * 7 + [_resident(rk.shape), _resident(lnw.shape), _resident(lnb.shape)],
        out_specs=tok_spec,
        out_shape=jax.ShapeDtypeStruct((b, s, dim), _BF16),
        scratch_shapes=[pltpu.VMEM((dim // WKV_GROUP, WKV_GROUP, WKV_GROUP), _F32)],
        compiler_params=pltpu.CompilerParams(
            dimension_semantics=("parallel", "arbitrary"), vmem_limit_bytes=VMEM_LIMIT),
        name="wkv7",
    )(r, ld, k, v, kk, bvec, g, rk, lnw, lnb)


def _stage_d_kernel(x_ref, oa_ref, ob_ref, p_ref, gmix_ref, wgate_ref, woa_ref, wob_ref, wout_ref,
                    gffn_ref, wup_ref, wdown_ref, gple_ref, wpg_ref, wpp_ref, gfin_ref, out_ref, *,
                    final_norm):
    x = x_ref[...]
    h = _rms(x, gmix_ref[...]).astype(_BF16)
    gate = jax.nn.sigmoid(_dot(h, wgate_ref[...]))
    y_a = _dot(oa_ref[...], woa_ref[...])
    y_b = _dot(ob_ref[...], wob_ref[...])
    mix = gate[:, :D_MODEL] * y_a + gate[:, D_MODEL:] * y_b
    x = x + _dot(mix.astype(_BF16), wout_ref[...])
    h = _rms(x, gffn_ref[...]).astype(_BF16)
    for c0 in range(0, D_FF, FF_CHUNK):
        hid = jnp.square(jnp.maximum(_dot(h, wup_ref[:, c0:c0 + FF_CHUNK]), 0.0))
        x = x + _dot(hid.astype(_BF16), wdown_ref[c0:c0 + FF_CHUNK, :])
    ple_gate = jax.nn.sigmoid(_dot(_rms(x, gple_ref[...]).astype(_BF16), wpg_ref[...]))
    x = x + ple_gate * _dot(p_ref[...].astype(_BF16), wpp_ref[...])
    out_ref[...] = _rms(x, gfin_ref[...]) if final_norm else x


def _stage_d(x, oa, ob, p, gmix, wgate, woa, wob, wout, gffn, wup, wdown, gple, wpg, wpp, gfin,
             final_norm):
    t = x.shape[0]
    tm = min(TM_D, t)
    tok = lambda i: (i, 0)
    weights = (gmix, wgate, woa, wob, wout, gffn, wup, wdown, gple, wpg, wpp, gfin)
    return pl.pallas_call(
        functools.partial(_stage_d_kernel, final_norm=final_norm),
        grid=(t // tm,),
        in_specs=[pl.BlockSpec((tm, D_MODEL), tok),
                  pl.BlockSpec((tm, oa.shape[1]), tok),
                  pl.BlockSpec((tm, ob.shape[1]), tok),
                  pl.BlockSpec((tm, PLE_DIM), tok)] + [_resident(w.shape) for w in weights],
        out_specs=pl.BlockSpec((tm, D_MODEL), tok),
        out_shape=jax.ShapeDtypeStruct((t, D_MODEL), _F32),
        compiler_params=pltpu.CompilerParams(
            dimension_semantics=("parallel",), vmem_limit_bytes=VMEM_LIMIT),
        name="stage_d",
    )(x, oa, ob, p, *weights)


def _rotate_half(w_pe):
    half = w_pe.shape[-1] // 2
    return jnp.concatenate([-w_pe[..., half:], w_pe[..., :half]], axis=-1)


def _prepare_layer(i, g_mix, w_in, g_q_a, w_uq, g_kv_a, w_ukv, w_o_mla, mu_rwkv, w0, w2, a0, a2,
                   g2, k_k, k_a, r_k, ln_x_w, ln_x_b, w_o_rwkv, w_out, g_ffn, w_ffn_up, w_ffn_down,
                   g_ple, w_ple_gate, w_ple_proj):
    row = lambda a: a[i].reshape(1, -1)
    w = w_in[i].astype(_BF16)
    kpe = w[:, Q_LORA_RANK + KV_LORA_RANK:MLA_COLS]
    kpe_slab = jnp.pad(jnp.concatenate([kpe, _rotate_half(kpe)], axis=1),
                       ((0, 0), (0, HEAD_SLAB - 2 * QK_ROPE_DIM)))
    win = jnp.concatenate([w[:, :Q_LORA_RANK + KV_LORA_RANK], kpe_slab,
                           w[:, MLA_COLS:MLA_COLS + RWKV_COLS]], axis=1)
    wgate = w[:, MLA_COLS + RWKV_COLS:]
    uq = w_uq[i].astype(_BF16)
    wq = uq.T
    uq_pe = uq.reshape(-1, MLA_HEADS, QK_HEAD_DIM)[:, :, QK_NOPE_DIM:]
    wqrot = _rotate_half(uq_pe).reshape(-1, MLA_HEADS * QK_ROPE_DIM).T
    ukv = w_ukv[i].astype(_BF16).reshape(-1, MLA_HEADS, QK_NOPE_DIM + V_HEAD_DIM)
    wk = jnp.pad(ukv[:, :, :QK_NOPE_DIM], ((0, 0), (0, 0), (0, HEAD_SLAB - QK_NOPE_DIM))).reshape(
        -1, MLA_HEADS * HEAD_SLAB)
    wv = ukv[:, :, QK_NOPE_DIM:].reshape(-1, MLA_HEADS * V_HEAD_DIM).T
    wwa = jnp.concatenate(
        [jnp.concatenate([w2[i], jnp.zeros_like(w2[i])], axis=1),
         jnp.concatenate([jnp.zeros_like(a2[i]), a2[i]], axis=1)], axis=0).astype(_BF16)
    stage_a = (row(g_mix), win, row(g_q_a), wq, wqrot, row(g_kv_a), wk, wv, row(mu_rwkv), row(w0),
               wwa, row(a0), g2[i].astype(_BF16), row(k_k), row(k_a))
    wkv = (r_k[i].reshape(1, -1), row(ln_x_w), row(ln_x_b))
    stage_d = (row(g_mix), wgate, w_o_mla[i].astype(_BF16), w_o_rwkv[i].astype(_BF16),
               w_out[i].astype(_BF16), row(g_ffn), w_ffn_up[i].astype(_BF16),
               w_ffn_down[i].astype(_BF16), row(g_ple), w_ple_gate[i].astype(_BF16),
               w_ple_proj[i].astype(_BF16))
    return stage_a, wkv, stage_d


def kernel(x, p, positions, g_mix, w_in, g_q_a, w_uq, g_kv_a, w_ukv, w_o_mla, mu_rwkv, w0, w2, a0, a2, g2, k_k, k_a, r_k, ln_x_w, ln_x_b, w_o_rwkv, w_out, g_ffn, w_ffn_up, w_ffn_down, g_ple, w_ple_gate, w_ple_proj, g_final):
    b, s, d = x.shape
    depth = w_in.shape[0]
    t = b * s
    cos_t, sin_t = _rope_tables(positions)
    gfin = g_final.reshape(1, -1)
    for i in range(depth):
        sa, sw, sd = _prepare_layer(i, g_mix, w_in, g_q_a, w_uq, g_kv_a, w_ukv, w_o_mla, mu_rwkv,
                                    w0, w2, a0, a2, g2, k_k, k_a, r_k, ln_x_w, ln_x_b, w_o_rwkv,
                                    w_out, g_ffn, w_ffn_up, w_ffn_down, g_ple, w_ple_gate,
                                    w_ple_proj)
        q, k, v, r, ld, kmod, vr, kkn, bvec, g = _stage_a(x, cos_t, sin_t, *sa)
        o_a = _attention(q, k, v)
        o_b = _wkv(r, ld, kmod, vr, kkn, bvec, g, *sw)
        out = _stage_d(x.reshape(t, d), o_a.reshape(t, -1), o_b.reshape(t, -1), p[i].reshape(t, -1),
                       *sd, gfin, final_norm=(i == depth - 1))
        x = out.reshape(b, s, d)
    return x
```

```python
import functools
import math

import jax
import jax.numpy as jnp
from jax import lax
from jax.experimental import pallas as pl
from jax.experimental.pallas import tpu as pltpu

D_MODEL = 1024
MLA_HEADS = 8
QK_NOPE_DIM = 64
QK_ROPE_DIM = 32
QK_HEAD_DIM = QK_NOPE_DIM + QK_ROPE_DIM
V_HEAD_DIM = 64
Q_LORA_RANK = 384
KV_LORA_RANK = 256
ROPE_THETA = 10000.0
RWKV_HEADS = 8
RWKV_HEAD_DIM = 64
RWKV_DIM = RWKV_HEADS * RWKV_HEAD_DIM
DECAY_LORA = 64
AAA_LORA = 64
GATE_LORA = 128
GN_EPS = RWKV_HEAD_DIM * 1e-5
MLA_COLS = Q_LORA_RANK + KV_LORA_RANK + QK_ROPE_DIM
RWKV_COLS = 3 * RWKV_DIM + DECAY_LORA + AAA_LORA + GATE_LORA
D_FF = 4 * D_MODEL
PLE_DIM = 256
RMS_EPS = 1e-6
LOG2_E = 1.4426950408889634
MASK_VALUE = -1e30
DECAY_SCALE = -math.exp(-0.5)
KK_NORM_FLOOR = 1e-12

LANES = 128
HEAD_SLAB = LANES
VMEM_LIMIT = 56 * 1024 * 1024

TM_A = 512
TQ = 512
ATTN_HEADS = 8
ATTN_SUM_ROWS = 16
WKV_CHUNK = 64
WKV_BLOCK = 512
WKV_GROUP = 256
TM_D = 512
FF_CHUNK = 1024

A_CQ = 0
A_CKV = A_CQ + Q_LORA_RANK
A_KPE = A_CKV + KV_LORA_RANK
A_RWKV = A_KPE + HEAD_SLAB
A_COLS = A_RWKV + RWKV_COLS

_BF16 = jnp.bfloat16
_F32 = jnp.float32


def _dot(a, b):
    return jnp.dot(a, b, preferred_element_type=_F32)


def _dot_nt(a, b):
    return lax.dot_general(a, b, (((1,), (1,)), ((), ())), preferred_element_type=_F32)


def _dot_tn(a, b):
    return lax.dot_general(a, b, (((0,), (0,)), ((), ())), preferred_element_type=_F32)


def _rms(x, g):
    return x * lax.rsqrt(jnp.mean(x * x, axis=-1, keepdims=True) + RMS_EPS) * g


def _same_head(shape):
    row = lax.broadcasted_iota(jnp.int32, shape, len(shape) - 2) // RWKV_HEAD_DIM
    col = lax.broadcasted_iota(jnp.int32, shape, len(shape) - 1) // RWKV_HEAD_DIM
    return row == col


def _head_sums(x):
    ones_bd = jnp.where(_same_head((WKV_GROUP, WKV_GROUP)), 1.0, 0.0).astype(_BF16)
    xb = x.astype(_BF16)
    return jnp.concatenate([_dot(xb[:, g0:g0 + WKV_GROUP], ones_bd)
                            for g0 in range(0, x.shape[1], WKV_GROUP)], axis=1)


def _resident(shape):
    zeros = (0,) * len(shape)
    return pl.BlockSpec(shape, lambda *_: zeros, pipeline_mode=pl.Buffered(1))


def _rope_kernel(pos_ref, inv_ref, cos_ref, sin_ref):
    ang = pos_ref[...] * inv_ref[...]
    cos_ref[...] = jnp.cos(ang)
    sin_ref[...] = jnp.sin(ang)


def _rope_tables(positions):
    half = QK_ROPE_DIM // 2
    t = positions.size
    inv_freq = ROPE_THETA ** (-jnp.arange(half, dtype=_F32) / half)
    return pl.pallas_call(
        _rope_kernel,
        out_shape=(jax.ShapeDtypeStruct((half, t), _F32),) * 2,
        name="rope_tables",
    )(positions.reshape(1, t).astype(_F32), inv_freq.reshape(half, 1))


def _stage_a_kernel(x_ref, cos_t_ref, sin_t_ref,
                    gmix_ref, win_ref, gq_ref, wq_ref, wqrot_ref,
                    gkv_ref, wk_ref, wv_ref, mu_ref, w0_ref, wwa_ref, a0_ref, g2_ref,
                    kk_ref, ka_ref,
                    q_out, k_out, v_out, r_out, ld_out, kmod_out, vr_out, kkn_out, b_out, g_out,
                    carry_ref):
    tm = x_ref.shape[1]

    @pl.when(pl.program_id(1) == 0)
    def _():
        carry_ref[...] = jnp.zeros_like(carry_ref)

    h = _rms(x_ref[0], gmix_ref[...]).astype(_BF16)
    z = _dot(h, win_ref[...])

    cos2 = jnp.concatenate([cos_t_ref[...]] * 2, axis=0)
    sin2 = jnp.concatenate([sin_t_ref[...]] * 2, axis=0)
    cq = _rms(z[:, A_CQ:A_CQ + Q_LORA_RANK], gq_ref[...]).astype(_BF16)
    q_all = _dot_nt(wq_ref[...], cq)
    q_rot = _dot_nt(wqrot_ref[...], cq)
    q_scale = QK_HEAD_DIM ** -0.5 * LOG2_E
    pad = jnp.zeros((HEAD_SLAB - QK_HEAD_DIM, tm), _BF16)
    rows = []
    for hd in range(MLA_HEADS):
        base = hd * QK_HEAD_DIM
        nope = q_all[base:base + QK_NOPE_DIM]
        rope = (q_all[base + QK_NOPE_DIM:base + QK_HEAD_DIM] * cos2
                + q_rot[hd * QK_ROPE_DIM:(hd + 1) * QK_ROPE_DIM] * sin2)
        rows += [(nope * q_scale).astype(_BF16), (rope * q_scale).astype(_BF16), pad]
    q_out[0] = jnp.concatenate(rows, axis=0)
    ckv = _rms(z[:, A_CKV:A_CKV + KV_LORA_RANK], gkv_ref[...]).astype(_BF16)
    kpe_t = z[:, A_KPE:A_KPE + HEAD_SLAB].T
    kpe_t = kpe_t[:QK_ROPE_DIM] * cos2 + kpe_t[QK_ROPE_DIM:2 * QK_ROPE_DIM] * sin2
    kpe = jnp.concatenate([jnp.zeros((QK_NOPE_DIM, tm), _F32), kpe_t,
                           jnp.zeros((HEAD_SLAB - QK_HEAD_DIM, tm), _F32)], axis=0).T
    k = _dot(ckv, wk_ref[...]) + jnp.concatenate([kpe] * MLA_HEADS, axis=1)
    k_out[0] = k.astype(_BF16)
    v_out[0] = _dot_nt(wv_ref[...], ckv).astype(_BF16)

    zr = z[:, A_RWKV:A_RWKV + RWKV_COLS]
    row = lax.broadcasted_iota(jnp.int32, zr.shape, 0)
    prev = jnp.where(row == 0, carry_ref[...], pltpu.roll(zr, shift=1, axis=0))
    carry_ref[...] = zr[tm - 1:tm, :]
    zs = zr + (prev - zr) * mu_ref[...]
    r = zs[:, 0:RWKV_DIM]
    k_raw = zs[:, RWKV_DIM:2 * RWKV_DIM]
    v_r = zs[:, 2 * RWKV_DIM:3 * RWKV_DIM]
    lo = zs[:, 3 * RWKV_DIM:3 * RWKV_DIM + LANES]
    g_lo = zs[:, 3 * RWKV_DIM + LANES:]
    lane = lax.broadcasted_iota(jnp.int32, lo.shape, 1)
    lo = jnp.where(lane < DECAY_LORA, jnp.tanh(lo), lo).astype(_BF16)
    wa = _dot(lo, wwa_ref[...])
    log_decay = DECAY_SCALE * jax.nn.sigmoid(w0_ref[...] + wa[:, :RWKV_DIM])
    a = jax.nn.sigmoid(a0_ref[...] + wa[:, RWKV_DIM:])
    g = _dot(jax.nn.sigmoid(g_lo).astype(_BF16), g2_ref[...])
    kk = k_raw * kk_ref[...]
    kkn = kk * lax.rsqrt(jnp.maximum(_head_sums(kk * kk), KK_NORM_FLOOR ** 2))
    r_out[0] = r
    ld_out[0] = log_decay
    kmod_out[0] = k_raw * (1.0 + (a - 1.0) * ka_ref[...])
    vr_out[0] = v_r
    kkn_out[0] = kkn
    b_out[0] = kkn * a
    g_out[0] = g


def _stage_a(x, cos_t, sin_t, gmix, win, gq, wq, wqrot, gkv, wk, wv, mu, w0, wwa, a0,
             g2, kk, ka):
    b, s, _ = x.shape
    tm = min(TM_A, s)
    tok = lambda bi, si: (bi, si, 0)
    tok_t = lambda bi, si: (bi, 0, si)
    flat_t = lambda bi, si: (0, bi * (s // tm) + si)
    rwkv_tok = jax.ShapeDtypeStruct((b, s, RWKV_DIM), _F32)
    rwkv_spec = pl.BlockSpec((1, tm, RWKV_DIM), tok)
    weights = (gmix, win, gq, wq, wqrot, gkv, wk, wv, mu, w0, wwa, a0, g2, kk, ka)
    return pl.pallas_call(
        _stage_a_kernel,
        grid=(b, s // tm),
        in_specs=[pl.BlockSpec((1, tm, D_MODEL), tok),
                  pl.BlockSpec((cos_t.shape[0], tm), flat_t),
                  pl.BlockSpec((sin_t.shape[0], tm), flat_t)]
                 + [_resident(w.shape) for w in weights],
        out_specs=[pl.BlockSpec((1, MLA_HEADS * HEAD_SLAB, tm), tok_t),
                   pl.BlockSpec((1, tm, MLA_HEADS * HEAD_SLAB), tok),
                   pl.BlockSpec((1, MLA_HEADS * V_HEAD_DIM, tm), tok_t)]
                  + [rwkv_spec] * 7,
        out_shape=[jax.ShapeDtypeStruct((b, MLA_HEADS * HEAD_SLAB, s), _BF16),
                   jax.ShapeDtypeStruct((b, s, MLA_HEADS * HEAD_SLAB), _BF16),
                   jax.ShapeDtypeStruct((b, MLA_HEADS * V_HEAD_DIM, s), _BF16)] + [rwkv_tok] * 7,
        scratch_shapes=[pltpu.VMEM((1, RWKV_COLS), _F32)],
        compiler_params=pltpu.CompilerParams(
            dimension_semantics=("arbitrary", "arbitrary"), vmem_limit_bytes=VMEM_LIMIT),
        name="stage_a",
    )(x, cos_t, sin_t, *weights)


def _attn_kernel(q_ref, k_ref, vt_ref, o_ref, s_a, s_b, mx_a, mx_b, m_ref, acc_ref):
    tq = q_ref.shape[2]
    nh = q_ref.shape[1] // HEAD_SLAB
    qi = pl.program_id(2)
    key = lax.broadcasted_iota(jnp.int32, (tq, tq), 0)
    qry = lax.broadcasted_iota(jnp.int32, (tq, tq), 1)
    causal = key <= qry
    ones = jnp.ones((ATTN_SUM_ROWS, tq), _BF16)

    m_ref[...] = jnp.full(m_ref.shape, MASK_VALUE, _F32)
    acc_ref[...] = jnp.zeros(acc_ref.shape, _F32)

    def scores(kt, s_buf, mx_buf, masked=False):
        off = pl.multiple_of(kt * tq, tq)
        for hh in range(nh):
            s = _dot(k_ref[0, pl.ds(off, tq), hh * HEAD_SLAB:(hh + 1) * HEAD_SLAB],
                     q_ref[0, hh * HEAD_SLAB:(hh + 1) * HEAD_SLAB, :])
            if masked:
                s = jnp.where(causal, s, MASK_VALUE)
            s_buf[hh] = s
            mx_buf[hh] = jnp.max(s, axis=0, keepdims=True)

    def consume(kt, s_buf, mx_buf):
        off = pl.multiple_of(kt * tq, tq)
        for hh in range(nh):
            m = m_ref[hh]
            m_new = jnp.maximum(m, mx_buf[hh])
            p = jnp.exp2(s_buf[hh] - m_new).astype(_BF16)
            vt = vt_ref[0, hh * V_HEAD_DIM:(hh + 1) * V_HEAD_DIM, pl.ds(off, tq)]
            acc_ref[hh] = (jnp.exp2(m - m_new) * acc_ref[hh]
                           + _dot(jnp.concatenate([vt, ones], axis=0), p))
            m_ref[hh] = m_new

    def tile_at(i):
        return jnp.where(i == 0, qi, i - 1)

    scores(qi, s_a, mx_a, masked=True)

    def two_steps(i2, _):
        i = 2 * i2
        scores(i, s_b, mx_b)
        consume(tile_at(i), s_a, mx_a)
        scores(i + 1, s_a, mx_a)
        consume(i, s_b, mx_b)
        return 0

    lax.fori_loop(0, qi // 2, two_steps, 0)

    @pl.when(qi % 2 == 1)
    def _():
        scores(qi - 1, s_b, mx_b)
        consume(tile_at(qi - 1), s_a, mx_a)
        consume(qi - 1, s_b, mx_b)

    @pl.when(qi % 2 == 0)
    def _():
        consume(tile_at(qi), s_a, mx_a)

    outs = [acc_ref[hh, :V_HEAD_DIM] / acc_ref[hh, V_HEAD_DIM:V_HEAD_DIM + 1] for hh in range(nh)]
    o_ref[0] = jnp.concatenate(outs, axis=0).T.astype(_BF16)


def _attention(qt, k, vt):
    b, s, _ = k.shape
    tq = min(TQ, s)
    groups = MLA_HEADS // ATTN_HEADS
    qk_w = ATTN_HEADS * HEAD_SLAB
    v_w = ATTN_HEADS * V_HEAD_DIM
    return pl.pallas_call(
        _attn_kernel,
        grid=(b, groups, s // tq),
        in_specs=[pl.BlockSpec((1, qk_w, tq), lambda bi, hg, qi: (bi, hg, qi)),
                  pl.BlockSpec((1, s, qk_w), lambda bi, hg, qi: (bi, 0, hg),
                               pipeline_mode=pl.Buffered(1)),
                  pl.BlockSpec((1, v_w, s), lambda bi, hg, qi: (bi, hg, 0),
                               pipeline_mode=pl.Buffered(1))],
        out_specs=pl.BlockSpec((1, tq, v_w), lambda bi, hg, qi: (bi, qi, hg)),
        out_shape=jax.ShapeDtypeStruct((b, s, MLA_HEADS * V_HEAD_DIM), _BF16),
        scratch_shapes=[pltpu.VMEM((ATTN_HEADS, tq, tq), _F32),
                        pltpu.VMEM((ATTN_HEADS, tq, tq), _F32),
                        pltpu.VMEM((ATTN_HEADS, 1, tq), _F32),
                        pltpu.VMEM((ATTN_HEADS, 1, tq), _F32),
                        pltpu.VMEM((ATTN_HEADS, 1, tq), _F32),
                        pltpu.VMEM((ATTN_HEADS, V_HEAD_DIM + ATTN_SUM_ROWS, tq), _F32)],
        compiler_params=pltpu.CompilerParams(
            dimension_semantics=("parallel", "parallel", "arbitrary"),
            vmem_limit_bytes=VMEM_LIMIT),
        name="mla_attention",
    )(qt, k, vt)


def _bdot(a, b):
    return jnp.einsum("nij,njk->nik", a, b, preferred_element_type=_F32)


def _bdot_nt(a, b):
    return jnp.einsum("nik,njk->nij", a, b, preferred_element_type=_F32)


def _bdot_tn(a, b):
    return jnp.einsum("nki,nkj->nij", a, b, preferred_element_type=_F32)


def _wkv_chunk_terms(r, ld, k, v, kk, b):
    n, c, gw = r.shape
    heads = gw // RWKV_HEAD_DIM
    row = lax.broadcasted_iota(jnp.int32, (c, gw), 0)
    col = lax.broadcasted_iota(jnp.int32, (c, gw), 1) % c
    strict = (col < row)[None]
    incl = (col <= row)[None]
    diff = (row ^ col)[None]
    same_head = _same_head((gw, gw))[None]

    def blockdiag(x):
        return jnp.where(same_head, jnp.concatenate([x] * heads, axis=1), jnp.zeros((), x.dtype))

    trow = lax.broadcasted_iota(jnp.int32, (c, c), 0)
    tcol = lax.broadcasted_iota(jnp.int32, (c, c), 1)
    tri = jnp.broadcast_to((tcol <= trow).astype(_BF16)[None], (n, c, c))
    ld_hi = ld.astype(_BF16)
    ld_lo = (ld - ld_hi.astype(_F32)).astype(_BF16)
    cum = _bdot(tri, ld_hi) + _bdot(tri, ld_lo)
    cum_last = cum[:, c - 1:c, :]
    e_neg = jnp.exp(-cum)
    e_tail = jnp.exp(cum_last - cum)
    r_hat = r * jnp.exp(cum)
    a_hat_bf = (-kk * jnp.exp(cum - ld)).astype(_BF16)
    lhs = jnp.concatenate([a_hat_bf, r_hat.astype(_BF16)], axis=1)
    v_bf = v.astype(_BF16)
    a_b = _bdot_nt(lhs, blockdiag((b * e_neg).astype(_BF16)))
    a_k = _bdot_nt(lhs, blockdiag((k * e_neg).astype(_BF16)))
    a_ab = jnp.where(strict, a_b[:, :c], 0.0)
    a_rb = jnp.where(incl, a_b[:, c:], 0.0).astype(_BF16)
    a_akrk = jnp.concatenate([jnp.where(strict, a_k[:, :c], 0.0),
                              jnp.where(incl, a_k[:, c:], 0.0)], axis=1).astype(_BF16)

    t = jnp.where(diff == 0, 1.0, 0.0) + jnp.where(diff == 1, a_ab, 0.0)
    size = 2
    while size < c:
        a_l = jnp.where((diff >= size) & (diff < 2 * size), a_ab, 0.0).astype(_BF16)
        t_bf = t.astype(_BF16)
        t = t + _bdot(_bdot(t_bf, blockdiag(a_l)).astype(_BF16), blockdiag(t_bf))
        size *= 2
    t_bf = t.astype(_BF16)

    v_part = _bdot(a_akrk, blockdiag(v_bf))
    w = _bdot(t_bf, blockdiag(a_hat_bf)).astype(_BF16)
    u0 = _bdot(t_bf, blockdiag(v_part[:, :c].astype(_BF16))).astype(_BF16)
    qe = r_hat + _bdot(a_rb, blockdiag(w))
    y0 = v_part[:, c:] + _bdot(a_rb, blockdiag(u0))
    b_bar = (b * e_tail).astype(_BF16)
    g = jnp.where(same_head, _bdot_tn(w, b_bar), 0.0)
    h = jnp.where(same_head,
                  _bdot_tn(jnp.concatenate([v_bf, u0], axis=1),
                           jnp.concatenate([(k * e_tail).astype(_BF16), b_bar], axis=1)), 0.0)
    return qe.astype(_BF16), y0, g.astype(_BF16), h, jnp.exp(cum_last)


def _wkv_kernel(r_ref, ld_ref, k_ref, v_ref, kk_ref, b_ref, g_ref, rk_ref, lnw_ref, lnb_ref,
                o_ref, state_ref):
    blk, dim = r_ref.shape[1:]
    c = WKV_CHUNK
    nc = blk // c
    groups = dim // WKV_GROUP

    @pl.when(pl.program_id(1) == 0)
    def _():
        state_ref[...] = jnp.zeros_like(state_ref)

    def problems(ref):
        x = ref[0].reshape(nc, c, dim)
        return jnp.concatenate([x[:, :, g0:g0 + WKV_GROUP] for g0 in range(0, dim, WKV_GROUP)],
                               axis=0)

    terms = _wkv_chunk_terms(problems(r_ref), problems(ld_ref), problems(k_ref), problems(v_ref),
                             problems(kk_ref), problems(b_ref))
    qe, y0, g, h, decay = (x.reshape((groups, nc) + x.shape[1:]) for x in terms)

    state = state_ref[...]
    ys = []
    for ci in range(nc):
        state_bf = state.astype(_BF16)
        ys.append(_bdot_nt(qe[:, ci], state_bf) + y0[:, ci])
        state = state * decay[:, ci] + _bdot(state_bf, g[:, ci]) + h[:, ci]
    state_ref[...] = state
    y = jnp.concatenate(ys, axis=1)
    y = jnp.concatenate([y[gi] for gi in range(groups)], axis=-1)

    inv_n = 1.0 / RWKV_HEAD_DIM
    centered = y - _head_sums(y) * inv_n
    var = _head_sums(centered * centered) * inv_n
    yn = centered * lax.rsqrt(var + GN_EPS) * lnw_ref[...] + lnb_ref[...]
    bonus = _head_sums(r_ref[0] * k_ref[0] * rk_ref[...]) * v_ref[0]
    o_ref[0] = ((yn + bonus) * g_ref[0]).astype(_BF16)


def _wkv(r, ld, k, v, kk, bvec, g, rk, lnw, lnb):
    b, s, dim = r.shape
    blk = min(WKV_BLOCK, s)
    tok_spec = pl.BlockSpec((1, blk, dim), lambda bi, si: (bi, si, 0))
    return pl.pallas_call(
        _wkv_kernel,
        grid=(b, s // blk),
        in_specs=[tok_spec] * 7 + [_resident(rk.shape), _resident(lnw.shape), _resident(lnb.shape)],
        out_specs=tok_spec,
        out_shape=jax.ShapeDtypeStruct((b, s, dim), _BF16),
        scratch_shapes=[pltpu.VMEM((dim // WKV_GROUP, WKV_GROUP, WKV_GROUP), _F32)],
        compiler_params=pltpu.CompilerParams(
            dimension_semantics=("parallel", "arbitrary"), vmem_limit_bytes=VMEM_LIMIT),
        name="wkv7",
    )(r, ld, k, v, kk, bvec, g, rk, lnw, lnb)


def _stage_d_kernel(x_ref, oa_ref, ob_ref, p_ref, gmix_ref, wgate_ref, woa_ref, wob_ref, wout_ref,
                    gffn_ref, wup_ref, wdown_ref, gple_ref, wpg_ref, wpp_ref, gfin_ref, out_ref, *,
                    final_norm):
    x = x_ref[...]
    h = _rms(x, gmix_ref[...]).astype(_BF16)
    gate = jax.nn.sigmoid(_dot(h, wgate_ref[...]))
    y_a = _dot(oa_ref[...], woa_ref[...])
    y_b = _dot(ob_ref[...], wob_ref[...])
    mix = gate[:, :D_MODEL] * y_a + gate[:, D_MODEL:] * y_b
    x = x + _dot(mix.astype(_BF16), wout_ref[...])
    h = _rms(x, gffn_ref[...]).astype(_BF16)
    for c0 in range(0, D_FF, FF_CHUNK):
        hid = jnp.square(jnp.maximum(_dot(h, wup_ref[:, c0:c0 + FF_CHUNK]), 0.0))
        x = x + _dot(hid.astype(_BF16), wdown_ref[c0:c0 + FF_CHUNK, :])
    ple_gate = jax.nn.sigmoid(_dot(_rms(x, gple_ref[...]).astype(_BF16), wpg_ref[...]))
    x = x + ple_gate * _dot(p_ref[...].astype(_BF16), wpp_ref[...])
    out_ref[...] = _rms(x, gfin_ref[...]) if final_norm else x


def _stage_d(x, oa, ob, p, gmix, wgate, woa, wob, wout, gffn, wup, wdown, gple, wpg, wpp, gfin,
             final_norm):
    t = x.shape[0]
    tm = min(TM_D, t)
    tok = lambda i: (i, 0)
    weights = (gmix, wgate, woa, wob, wout, gffn, wup, wdown, gple, wpg, wpp, gfin)
    return pl.pallas_call(
        functools.partial(_stage_d_kernel, final_norm=final_norm),
        grid=(t // tm,),
        in_specs=[pl.BlockSpec((tm, D_MODEL), tok),
                  pl.BlockSpec((tm, oa.shape[1]), tok),
                  pl.BlockSpec((tm, ob.shape[1]), tok),
                  pl.BlockSpec((tm, PLE_DIM), tok)] + [_resident(w.shape) for w in weights],
        out_specs=pl.BlockSpec((tm, D_MODEL), tok),
        out_shape=jax.ShapeDtypeStruct((t, D_MODEL), _F32),
        compiler_params=pltpu.CompilerParams(
            dimension_semantics=("parallel",), vmem_limit_bytes=VMEM_LIMIT),
        name="stage_d",
    )(x, oa, ob, p, *weights)


def _rotate_half(w_pe):
    half = w_pe.shape[-1] // 2
    return jnp.concatenate([-w_pe[..., half:], w_pe[..., :half]], axis=-1)


def _prepare_layer(i, g_mix, w_in, g_q_a, w_uq, g_kv_a, w_ukv, w_o_mla, mu_rwkv, w0, w2, a0, a2,
                   g2, k_k, k_a, r_k, ln_x_w, ln_x_b, w_o_rwkv, w_out, g_ffn, w_ffn_up, w_ffn_down,
                   g_ple, w_ple_gate, w_ple_proj):
    row = lambda a: a[i].reshape(1, -1)
    w = w_in[i].astype(_BF16)
    kpe = w[:, Q_LORA_RANK + KV_LORA_RANK:MLA_COLS]
    kpe_slab = jnp.pad(jnp.concatenate([kpe, _rotate_half(kpe)], axis=1),
                       ((0, 0), (0, HEAD_SLAB - 2 * QK_ROPE_DIM)))
    win = jnp.concatenate([w[:, :Q_LORA_RANK + KV_LORA_RANK], kpe_slab,
                           w[:, MLA_COLS:MLA_COLS + RWKV_COLS]], axis=1)
    wgate = w[:, MLA_COLS + RWKV_COLS:]
    uq = w_uq[i].astype(_BF16)
    wq = uq.T
    uq_pe = uq.reshape(-1, MLA_HEADS, QK_HEAD_DIM)[:, :, QK_NOPE_DIM:]
    wqrot = _rotate_half(uq_pe).reshape(-1, MLA_HEADS * QK_ROPE_DIM).T
    ukv = w_ukv[i].astype(_BF16).reshape(-1, MLA_HEADS, QK_NOPE_DIM + V_HEAD_DIM)
    wk = jnp.pad(ukv[:, :, :QK_NOPE_DIM], ((0, 0), (0, 0), (0, HEAD_SLAB - QK_NOPE_DIM))).reshape(
        -1, MLA_HEADS * HEAD_SLAB)
    wv = ukv[:, :, QK_NOPE_DIM:].reshape(-1, MLA_HEADS * V_HEAD_DIM).T
    wwa = jnp.concatenate(
        [jnp.concatenate([w2[i], jnp.zeros_like(w2[i])], axis=1),
         jnp.concatenate([jnp.zeros_like(a2[i]), a2[i]], axis=1)], axis=0).astype(_BF16)
    stage_a = (row(g_mix), win, row(g_q_a), wq, wqrot, row(g_kv_a), wk, wv, row(mu_rwkv), row(w0),
               wwa, row(a0), g2[i].astype(_BF16), row(k_k), row(k_a))
    wkv = (r_k[i].reshape(1, -1), row(ln_x_w), row(ln_x_b))
    stage_d = (row(g_mix), wgate, w_o_mla[i].astype(_BF16), w_o_rwkv[i].astype(_BF16),
               w_out[i].astype(_BF16), row(g_ffn), w_ffn_up[i].astype(_BF16),
               w_ffn_down[i].astype(_BF16), row(g_ple), w_ple_gate[i].astype(_BF16),
               w_ple_proj[i].astype(_BF16))
    return stage_a, wkv, stage_d


def kernel(x, p, positions, g_mix, w_in, g_q_a, w_uq, g_kv_a, w_ukv, w_o_mla, mu_rwkv, w0, w2, a0, a2, g2, k_k, k_a, r_k, ln_x_w, ln_x_b, w_o_rwkv, w_out, g_ffn, w_ffn_up, w_ffn_down, g_ple, w_ple_gate, w_ple_proj, g_final):
    b, s, d = x.shape
    depth = w_in.shape[0]
    t = b * s
    cos_t, sin_t = _rope_tables(positions)
    gfin = g_final.reshape(1, -1)
    for i in range(depth):
        sa, sw, sd = _prepare_layer(i, g_mix, w_in, g_q_a, w_uq, g_kv_a, w_ukv, w_o_mla, mu_rwkv,
                                    w0, w2, a0, a2, g2, k_k, k_a, r_k, ln_x_w, ln_x_b, w_o_rwkv,
                                    w_out, g_ffn, w_ffn_up, w_ffn_down, g_ple, w_ple_gate,
                                    w_ple_proj)
        q, k, v, r, ld, kmod, vr, kkn, bvec, g = _stage_a(x, cos_t, sin_t, *sa)
        o_a = _attention(q, k, v)
        o_b = _wkv(r, ld, kmod, vr, kkn, bvec, g, *sw)
        out = _stage_d(x.reshape(t, d), o_a.reshape(t, -1), o_b.reshape(t, -1), p[i].reshape(t, -1),
                       *sd, gfin, final_norm=(i == depth - 1))
        x = out.reshape(b, s, d)
    return x
```

```python
import functools
import math

import jax
import jax.numpy as jnp
from jax import lax
from jax.experimental import pallas as pl
from jax.experimental.pallas import tpu as pltpu

D_MODEL = 1024
MLA_HEADS = 8
QK_NOPE_DIM = 64
QK_ROPE_DIM = 32
QK_HEAD_DIM = QK_NOPE_DIM + QK_ROPE_DIM
V_HEAD_DIM = 64
Q_LORA_RANK = 384
KV_LORA_RANK = 256
ROPE_THETA = 10000.0
RWKV_HEADS = 8
RWKV_HEAD_DIM = 64
RWKV_DIM = RWKV_HEADS * RWKV_HEAD_DIM
DECAY_LORA = 64
AAA_LORA = 64
GATE_LORA = 128
GN_EPS = RWKV_HEAD_DIM * 1e-5
MLA_COLS = Q_LORA_RANK + KV_LORA_RANK + QK_ROPE_DIM
RWKV_COLS = 3 * RWKV_DIM + DECAY_LORA + AAA_LORA + GATE_LORA
D_FF = 4 * D_MODEL
PLE_DIM = 256
RMS_EPS = 1e-6
LOG2_E = 1.4426950408889634
MASK_VALUE = -1e30
DECAY_SCALE = -math.exp(-0.5)
KK_NORM_FLOOR = 1e-12

LANES = 128
HEAD_SLAB = LANES
VMEM_LIMIT = 56 * 1024 * 1024

TM_A = 512
TQ = 512
ATTN_HEADS = 4
ATTN_SUM_ROWS = 16
WKV_CHUNK = 64
WKV_BLOCK = 512
WKV_GROUP = 256
TM_D = 512
FF_CHUNK = 1024

A_CQ = 0
A_CKV = A_CQ + Q_LORA_RANK
A_KPE = A_CKV + KV_LORA_RANK
A_RWKV = A_KPE + HEAD_SLAB
A_COLS = A_RWKV + RWKV_COLS

_BF16 = jnp.bfloat16
_F32 = jnp.float32


def _dot(a, b):
    return jnp.dot(a, b, preferred_element_type=_F32)


def _dot_nt(a, b):
    return lax.dot_general(a, b, (((1,), (1,)), ((), ())), preferred_element_type=_F32)


def _dot_tn(a, b):
    return lax.dot_general(a, b, (((0,), (0,)), ((), ())), preferred_element_type=_F32)


def _rms(x, g):
    return x * lax.rsqrt(jnp.mean(x * x, axis=-1, keepdims=True) + RMS_EPS) * g


def _same_head(shape):
    row = lax.broadcasted_iota(jnp.int32, shape, len(shape) - 2) // RWKV_HEAD_DIM
    col = lax.broadcasted_iota(jnp.int32, shape, len(shape) - 1) // RWKV_HEAD_DIM
    return row == col


def _head_sums(x):
    ones_bd = jnp.where(_same_head((WKV_GROUP, WKV_GROUP)), 1.0, 0.0).astype(_BF16)
    xb = x.astype(_BF16)
    return jnp.concatenate([_dot(xb[:, g0:g0 + WKV_GROUP], ones_bd)
                            for g0 in range(0, x.shape[1], WKV_GROUP)], axis=1)


def _resident(shape):
    zeros = (0,) * len(shape)
    return pl.BlockSpec(shape, lambda *_: zeros, pipeline_mode=pl.Buffered(1))


def _rope_kernel(pos_ref, inv_ref, cos_ref, sin_ref):
    ang = pos_ref[...] * inv_ref[...]
    cos_ref[...] = jnp.cos(ang)
    sin_ref[...] = jnp.sin(ang)


def _rope_tables(positions):
    half = QK_ROPE_DIM // 2
    t = positions.size
    inv_freq = ROPE_THETA ** (-jnp.arange(half, dtype=_F32) / half)
    return pl.pallas_call(
        _rope_kernel,
        out_shape=(jax.ShapeDtypeStruct((half, t), _F32),) * 2,
        name="rope_tables",
    )(positions.reshape(1, t).astype(_F32), inv_freq.reshape(half, 1))


def _stage_a_kernel(x_ref, cos_t_ref, sin_t_ref,
                    gmix_ref, win_ref, gq_ref, wq_ref, wqrot_ref,
                    gkv_ref, wk_ref, wv_ref, mu_ref, w0_ref, wwa_ref, a0_ref, g2_ref,
                    kk_ref, ka_ref,
                    q_out, k_out, v_out, r_out, ld_out, kmod_out, vr_out, kkn_out, b_out, g_out,
                    carry_ref):
    tm = x_ref.shape[1]

    @pl.when(pl.program_id(1) == 0)
    def _():
        carry_ref[...] = jnp.zeros_like(carry_ref)

    h = _rms(x_ref[0], gmix_ref[...]).astype(_BF16)
    z = _dot(h, win_ref[...])

    cos2 = jnp.concatenate([cos_t_ref[...]] * 2, axis=0)
    sin2 = jnp.concatenate([sin_t_ref[...]] * 2, axis=0)
    cq = _rms(z[:, A_CQ:A_CQ + Q_LORA_RANK], gq_ref[...]).astype(_BF16)
    q_all = _dot_nt(wq_ref[...], cq)
    q_rot = _dot_nt(wqrot_ref[...], cq)
    q_scale = QK_HEAD_DIM ** -0.5 * LOG2_E
    pad = jnp.zeros((HEAD_SLAB - QK_HEAD_DIM, tm), _BF16)
    rows = []
    for hd in range(MLA_HEADS):
        base = hd * QK_HEAD_DIM
        nope = q_all[base:base + QK_NOPE_DIM]
        rope = (q_all[base + QK_NOPE_DIM:base + QK_HEAD_DIM] * cos2
                + q_rot[hd * QK_ROPE_DIM:(hd + 1) * QK_ROPE_DIM] * sin2)
        rows += [(nope * q_scale).astype(_BF16), (rope * q_scale).astype(_BF16), pad]
    q_out[0] = jnp.concatenate(rows, axis=0)
    ckv = _rms(z[:, A_CKV:A_CKV + KV_LORA_RANK], gkv_ref[...]).astype(_BF16)
    kpe_t = z[:, A_KPE:A_KPE + HEAD_SLAB].T
    kpe_t = kpe_t[:QK_ROPE_DIM] * cos2 + kpe_t[QK_ROPE_DIM:2 * QK_ROPE_DIM] * sin2
    kpe = jnp.concatenate([jnp.zeros((QK_NOPE_DIM, tm), _F32), kpe_t,
                           jnp.zeros((HEAD_SLAB - QK_HEAD_DIM, tm), _F32)], axis=0).T
    k = _dot(ckv, wk_ref[...]) + jnp.concatenate([kpe] * MLA_HEADS, axis=1)
    k_out[0] = k.astype(_BF16)
    v_out[0] = _dot_nt(wv_ref[...], ckv).astype(_BF16)

    zr = z[:, A_RWKV:A_RWKV + RWKV_COLS]
    row = lax.broadcasted_iota(jnp.int32, zr.shape, 0)
    prev = jnp.where(row == 0, carry_ref[...], pltpu.roll(zr, shift=1, axis=0))
    carry_ref[...] = zr[tm - 1:tm, :]
    zs = zr + (prev - zr) * mu_ref[...]
    r = zs[:, 0:RWKV_DIM]
    k_raw = zs[:, RWKV_DIM:2 * RWKV_DIM]
    v_r = zs[:, 2 * RWKV_DIM:3 * RWKV_DIM]
    lo = zs[:, 3 * RWKV_DIM:3 * RWKV_DIM + LANES]
    g_lo = zs[:, 3 * RWKV_DIM + LANES:]
    lane = lax.broadcasted_iota(jnp.int32, lo.shape, 1)
    lo = jnp.where(lane < DECAY_LORA, jnp.tanh(lo), lo).astype(_BF16)
    wa = _dot(lo, wwa_ref[...])
    log_decay = DECAY_SCALE * jax.nn.sigmoid(w0_ref[...] + wa[:, :RWKV_DIM])
    a = jax.nn.sigmoid(a0_ref[...] + wa[:, RWKV_DIM:])
    g = _dot(jax.nn.sigmoid(g_lo).astype(_BF16), g2_ref[...])
    kk = k_raw * kk_ref[...]
    kkn = kk * lax.rsqrt(jnp.maximum(_head_sums(kk * kk), KK_NORM_FLOOR ** 2))
    r_out[0] = r
    ld_out[0] = log_decay
    kmod_out[0] = k_raw * (1.0 + (a - 1.0) * ka_ref[...])
    vr_out[0] = v_r
    kkn_out[0] = kkn
    b_out[0] = kkn * a
    g_out[0] = g


def _stage_a(x, cos_t, sin_t, gmix, win, gq, wq, wqrot, gkv, wk, wv, mu, w0, wwa, a0,
             g2, kk, ka):
    b, s, _ = x.shape
    tm = min(TM_A, s)
    tok = lambda bi, si: (bi, si, 0)
    tok_t = lambda bi, si: (bi, 0, si)
    flat_t = lambda bi, si: (0, bi * (s // tm) + si)
    rwkv_tok = jax.ShapeDtypeStruct((b, s, RWKV_DIM), _F32)
    rwkv_spec = pl.BlockSpec((1, tm, RWKV_DIM), tok)
    weights = (gmix, win, gq, wq, wqrot, gkv, wk, wv, mu, w0, wwa, a0, g2, kk, ka)
    return pl.pallas_call(
        _stage_a_kernel,
        grid=(b, s // tm),
        in_specs=[pl.BlockSpec((1, tm, D_MODEL), tok),
                  pl.BlockSpec((cos_t.shape[0], tm), flat_t),
                  pl.BlockSpec((sin_t.shape[0], tm), flat_t)]
                 + [_resident(w.shape) for w in weights],
        out_specs=[pl.BlockSpec((1, MLA_HEADS * HEAD_SLAB, tm), tok_t),
                   pl.BlockSpec((1, tm, MLA_HEADS * HEAD_SLAB), tok),
                   pl.BlockSpec((1, MLA_HEADS * V_HEAD_DIM, tm), tok_t)]
                  + [rwkv_spec] * 7,
        out_shape=[jax.ShapeDtypeStruct((b, MLA_HEADS * HEAD_SLAB, s), _BF16),
                   jax.ShapeDtypeStruct((b, s, MLA_HEADS * HEAD_SLAB), _BF16),
                   jax.ShapeDtypeStruct((b, MLA_HEADS * V_HEAD_DIM, s), _BF16)] + [rwkv_tok] * 7,
        scratch_shapes=[pltpu.VMEM((1, RWKV_COLS), _F32)],
        compiler_params=pltpu.CompilerParams(
            dimension_semantics=("arbitrary", "arbitrary"), vmem_limit_bytes=VMEM_LIMIT),
        name="stage_a",
    )(x, cos_t, sin_t, *weights)


def _attn_kernel(q_ref, k_ref, vt_ref, o_ref, s_a, s_b, mx_a, mx_b, m_ref, acc_ref):
    tq = q_ref.shape[2]
    nh = q_ref.shape[1] // HEAD_SLAB
    qi = pl.program_id(2)
    key = lax.broadcasted_iota(jnp.int32, (tq, tq), 0)
    qry = lax.broadcasted_iota(jnp.int32, (tq, tq), 1)
    causal = key <= qry
    ones = jnp.ones((ATTN_SUM_ROWS, tq), _BF16)

    m_ref[...] = jnp.full(m_ref.shape, MASK_VALUE, _F32)
    acc_ref[...] = jnp.zeros(acc_ref.shape, _F32)

    def scores(kt, s_buf, mx_buf, masked=False):
        off = pl.multiple_of(kt * tq, tq)
        for hh in range(nh):
            s = _dot(k_ref[0, pl.ds(off, tq), hh * HEAD_SLAB:(hh + 1) * HEAD_SLAB],
                     q_ref[0, hh * HEAD_SLAB:(hh + 1) * HEAD_SLAB, :])
            if masked:
                s = jnp.where(causal, s, MASK_VALUE)
            s_buf[hh] = s
            mx_buf[hh] = jnp.max(s, axis=0, keepdims=True)

    def consume(kt, s_buf, mx_buf):
        off = pl.multiple_of(kt * tq, tq)
        for hh in range(nh):
            m = m_ref[hh]
            m_new = jnp.maximum(m, mx_buf[hh])
            p = jnp.exp2(s_buf[hh] - m_new).astype(_BF16)
            vt = vt_ref[0, hh * V_HEAD_DIM:(hh + 1) * V_HEAD_DIM, pl.ds(off, tq)]
            acc_ref[hh] = (jnp.exp2(m - m_new) * acc_ref[hh]
                           + _dot(jnp.concatenate([vt, ones], axis=0), p))
            m_ref[hh] = m_new

    def tile_at(i):
        return jnp.where(i == 0, qi, i - 1)

    scores(qi, s_a, mx_a, masked=True)

    def two_steps(i2, _):
        i = 2 * i2
        scores(i, s_b, mx_b)
        consume(tile_at(i), s_a, mx_a)
        scores(i + 1, s_a, mx_a)
        consume(i, s_b, mx_b)
        return 0

    lax.fori_loop(0, qi // 2, two_steps, 0)

    @pl.when(qi % 2 == 1)
    def _():
        scores(qi - 1, s_b, mx_b)
        consume(tile_at(qi - 1), s_a, mx_a)
        consume(qi - 1, s_b, mx_b)

    @pl.when(qi % 2 == 0)
    def _():
        consume(tile_at(qi), s_a, mx_a)

    outs = [acc_ref[hh, :V_HEAD_DIM] / acc_ref[hh, V_HEAD_DIM:V_HEAD_DIM + 1] for hh in range(nh)]
    o_ref[0] = jnp.concatenate(outs, axis=0).T.astype(_BF16)


def _attention(qt, k, vt):
    b, s, _ = k.shape
    tq = min(TQ, s)
    groups = MLA_HEADS // ATTN_HEADS
    qk_w = ATTN_HEADS * HEAD_SLAB
    v_w = ATTN_HEADS * V_HEAD_DIM
    return pl.pallas_call(
        _attn_kernel,
        grid=(b, groups, s // tq),
        in_specs=[pl.BlockSpec((1, qk_w, tq), lambda bi, hg, qi: (bi, hg, qi)),
                  pl.BlockSpec((1, s, qk_w), lambda bi, hg, qi: (bi, 0, hg)),
                  pl.BlockSpec((1, v_w, s), lambda bi, hg, qi: (bi, hg, 0))],
        out_specs=pl.BlockSpec((1, tq, v_w), lambda bi, hg, qi: (bi, qi, hg)),
        out_shape=jax.ShapeDtypeStruct((b, s, MLA_HEADS * V_HEAD_DIM), _BF16),
        scratch_shapes=[pltpu.VMEM((ATTN_HEADS, tq, tq), _F32),
                        pltpu.VMEM((ATTN_HEADS, tq, tq), _F32),
                        pltpu.VMEM((ATTN_HEADS, 1, tq), _F32),
                        pltpu.VMEM((ATTN_HEADS, 1, tq), _F32),
                        pltpu.VMEM((ATTN_HEADS, 1, tq), _F32),
                        pltpu.VMEM((ATTN_HEADS, V_HEAD_DIM + ATTN_SUM_ROWS, tq), _F32)],
        compiler_params=pltpu.CompilerParams(
            dimension_semantics=("parallel", "parallel", "arbitrary"),
            vmem_limit_bytes=VMEM_LIMIT),
        name="mla_attention",
    )(qt, k, vt)


def _bdot(a, b):
    return jnp.einsum("nij,njk->nik", a, b, preferred_element_type=_F32)


def _bdot_nt(a, b):
    return jnp.einsum("nik,njk->nij", a, b, preferred_element_type=_F32)


def _bdot_tn(a, b):
    return jnp.einsum("nki,nkj->nij", a, b, preferred_element_type=_F32)


def _wkv_chunk_terms(r, ld, k, v, kk, b):
    n, c, gw = r.shape
    heads = gw // RWKV_HEAD_DIM
    row = lax.broadcasted_iota(jnp.int32, (c, gw), 0)
    col = lax.broadcasted_iota(jnp.int32, (c, gw), 1) % c
    strict = (col < row)[None]
    incl = (col <= row)[None]
    diff = (row ^ col)[None]
    same_head = _same_head((gw, gw))[None]

    def blockdiag(x):
        return jnp.where(same_head, jnp.concatenate([x] * heads, axis=1), jnp.zeros((), x.dtype))

    trow = lax.broadcasted_iota(jnp.int32, (c, c), 0)
    tcol = lax.broadcasted_iota(jnp.int32, (c, c), 1)
    tri = jnp.broadcast_to((tcol <= trow).astype(_BF16)[None], (n, c, c))
    ld_hi = ld.astype(_BF16)
    ld_lo = (ld - ld_hi.astype(_F32)).astype(_BF16)
    cum = _bdot(tri, ld_hi) + _bdot(tri, ld_lo)
    cum_last = cum[:, c - 1:c, :]
    e_neg = jnp.exp(-cum)
    e_tail = jnp.exp(cum_last - cum)
    r_hat = r * jnp.exp(cum)
    a_hat_bf = (-kk * jnp.exp(cum - ld)).astype(_BF16)
    lhs = jnp.concatenate([a_hat_bf, r_hat.astype(_BF16)], axis=1)
    v_bf = v.astype(_BF16)
    a_b = _bdot_nt(lhs, blockdiag((b * e_neg).astype(_BF16)))
    a_k = _bdot_nt(lhs, blockdiag((k * e_neg).astype(_BF16)))
    a_ab = jnp.where(strict, a_b[:, :c], 0.0)
    a_rb = jnp.where(incl, a_b[:, c:], 0.0).astype(_BF16)
    a_akrk = jnp.concatenate([jnp.where(strict, a_k[:, :c], 0.0),
                              jnp.where(incl, a_k[:, c:], 0.0)], axis=1).astype(_BF16)

    t = jnp.where(diff == 0, 1.0, 0.0) + jnp.where(diff == 1, a_ab, 0.0)
    size = 2
    while size < c:
        a_l = jnp.where((diff >= size) & (diff < 2 * size), a_ab, 0.0).astype(_BF16)
        t_bf = t.astype(_BF16)
        t = t + _bdot(_bdot(t_bf, blockdiag(a_l)).astype(_BF16), blockdiag(t_bf))
        size *= 2
    t_bf = t.astype(_BF16)

    v_part = _bdot(a_akrk, blockdiag(v_bf))
    w = _bdot(t_bf, blockdiag(a_hat_bf)).astype(_BF16)
    u0 = _bdot(t_bf, blockdiag(v_part[:, :c].astype(_BF16))).astype(_BF16)
    qe = r_hat + _bdot(a_rb, blockdiag(w))
    y0 = v_part[:, c:] + _bdot(a_rb, blockdiag(u0))
    b_bar = (b * e_tail).astype(_BF16)
    g = jnp.where(same_head, _bdot_tn(w, b_bar), 0.0)
    h = jnp.where(same_head,
                  _bdot_tn(jnp.concatenate([v_bf, u0], axis=1),
                           jnp.concatenate([(k * e_tail).astype(_BF16), b_bar], axis=1)), 0.0)
    return qe.astype(_BF16), y0, g.astype(_BF16), h, jnp.exp(cum_last)


def _wkv_kernel(r_ref, ld_ref, k_ref, v_ref, kk_ref, b_ref, g_ref, rk_ref, lnw_ref, lnb_ref,
                o_ref, state_ref):
    blk, dim = r_ref.shape[1:]
    c = WKV_CHUNK
    nc = blk // c
    groups = dim // WKV_GROUP

    @pl.when(pl.program_id(1) == 0)
    def _():
        state_ref[...] = jnp.zeros_like(state_ref)

    def problems(ref):
        x = ref[0].reshape(nc, c, dim)
        return jnp.concatenate([x[:, :, g0:g0 + WKV_GROUP] for g0 in range(0, dim, WKV_GROUP)],
                               axis=0)

    terms = _wkv_chunk_terms(problems(r_ref), problems(ld_ref), problems(k_ref), problems(v_ref),
                             problems(kk_ref), problems(b_ref))
    qe, y0, g, h, decay = (x.reshape((groups, nc) + x.shape[1:]) for x in terms)

    state = state_ref[...]
    ys = []
    for ci in range(nc):
        state_bf = state.astype(_BF16)
        ys.append(_bdot_nt(qe[:, ci], state_bf) + y0[:, ci])
        state = state * decay[:, ci] + _bdot(state_bf, g[:, ci]) + h[:, ci]
    state_ref[...] = state
    y = jnp.concatenate(ys, axis=1)
    y = jnp.concatenate([y[gi] for gi in range(groups)], axis=-1)

    inv_n = 1.0 / RWKV_HEAD_DIM
    centered = y - _head_sums(y) * inv_n
    var = _head_sums(centered * centered) * inv_n
    yn = centered * lax.rsqrt(var + GN_EPS) * lnw_ref[...] + lnb_ref[...]
    bonus = _head_sums(r_ref[0] * k_ref[0] * rk_ref[...]) * v_ref[0]
    o_ref[0] = ((yn + bonus) * g_ref[0]).astype(_BF16)


def _wkv(r, ld, k, v, kk, bvec, g, rk, lnw, lnb):
    b, s, dim = r.shape
    blk = min(WKV_BLOCK, s)
    tok_spec = pl.BlockSpec((1, blk, dim), lambda bi, si: (bi, si, 0))
    return pl.pallas_call(
        _wkv_kernel,
        grid=(b, s // blk),
        in_specs=[tok_spec] * 7 + [_resident(rk.shape), _resident(lnw.shape), _resident(lnb.shape)],
        out_specs=tok_spec,
        out_shape=jax.ShapeDtypeStruct((b, s, dim), _BF16),
        scratch_shapes=[pltpu.VMEM((dim // WKV_GROUP, WKV_GROUP, WKV_GROUP), _F32)],
        compiler_params=pltpu.CompilerParams(
            dimension_semantics=("parallel", "arbitrary"), vmem_limit_bytes=VMEM_LIMIT),
        name="wkv7",
    )(r, ld, k, v, kk, bvec, g, rk, lnw, lnb)


def _stage_d_kernel(x_ref, oa_ref, ob_ref, p_ref, gmix_ref, wgate_ref, woa_ref, wob_ref, wout_ref,
                    gffn_ref, wup_ref, wdown_ref, gple_ref, wpg_ref, wpp_ref, gfin_ref, out_ref, *,
                    final_norm):
    x = x_ref[...]
    h = _rms(x, gmix_ref[...]).astype(_BF16)
    gate = jax.nn.sigmoid(_dot(h, wgate_ref[...]))
    y_a = _dot(oa_ref[...], woa_ref[...])
    y_b = _dot(ob_ref[...], wob_ref[...])
    mix = gate[:, :D_MODEL] * y_a + gate[:, D_MODEL:] * y_b
    x = x + _dot(mix.astype(_BF16), wout_ref[...])
    h = _rms(x, gffn_ref[...]).astype(_BF16)
    for c0 in range(0, D_FF, FF_CHUNK):
        hid = jnp.square(jnp.maximum(_dot(h, wup_ref[:, c0:c0 + FF_CHUNK]), 0.0))
        x = x + _dot(hid.astype(_BF16), wdown_ref[c0:c0 + FF_CHUNK, :])
    ple_gate = jax.nn.sigmoid(_dot(_rms(x, gple_ref[...]).astype(_BF16), wpg_ref[...]))
    x = x + ple_gate * _dot(p_ref[...].astype(_BF16), wpp_ref[...])
    out_ref[...] = _rms(x, gfin_ref[...]) if final_norm else x


def _stage_d(x, oa, ob, p, gmix, wgate, woa, wob, wout, gffn, wup, wdown, gple, wpg, wpp, gfin,
             final_norm):
    t = x.shape[0]
    tm = min(TM_D, t)
    tok = lambda i: (i, 0)
    weights = (gmix, wgate, woa, wob, wout, gffn, wup, wdown, gple, wpg, wpp, gfin)
    return pl.pallas_call(
        functools.partial(_stage_d_kernel, final_norm=final_norm),
        grid=(t // tm,),
        in_specs=[pl.BlockSpec((tm, D_MODEL), tok),
                  pl.BlockSpec((tm, oa.shape[1]), tok),
                  pl.BlockSpec((tm, ob.shape[1]), tok),
                  pl.BlockSpec((tm, PLE_DIM), tok)] + [_resident(w.shape) for w in weights],
        out_specs=pl.BlockSpec((tm, D_MODEL), tok),
        out_shape=jax.ShapeDtypeStruct((t, D_MODEL), _F32),
        compiler_params=pltpu.CompilerParams(
            dimension_semantics=("parallel",), vmem_limit_bytes=VMEM_LIMIT),
        name="stage_d",
    )(x, oa, ob, p, *weights)


def _rotate_half(w_pe):
    half = w_pe.shape[-1] // 2
    return jnp.concatenate([-w_pe[..., half:], w_pe[..., :half]], axis=-1)


def _prepare_layer(i, g_mix, w_in, g_q_a, w_uq, g_kv_a, w_ukv, w_o_mla, mu_rwkv, w0, w2, a0, a2,
                   g2, k_k, k_a, r_k, ln_x_w, ln_x_b, w_o_rwkv, w_out, g_ffn, w_ffn_up, w_ffn_down,
                   g_ple, w_ple_gate, w_ple_proj):
    row = lambda a: a[i].reshape(1, -1)
    w = w_in[i].astype(_BF16)
    kpe = w[:, Q_LORA_RANK + KV_LORA_RANK:MLA_COLS]
    kpe_slab = jnp.pad(jnp.concatenate([kpe, _rotate_half(kpe)], axis=1),
                       ((0, 0), (0, HEAD_SLAB - 2 * QK_ROPE_DIM)))
    win = jnp.concatenate([w[:, :Q_LORA_RANK + KV_LORA_RANK], kpe_slab,
                           w[:, MLA_COLS:MLA_COLS + RWKV_COLS]], axis=1)
    wgate = w[:, MLA_COLS + RWKV_COLS:]
    uq = w_uq[i].astype(_BF16)
    wq = uq.T
    uq_pe = uq.reshape(-1, MLA_HEADS, QK_HEAD_DIM)[:, :, QK_NOPE_DIM:]
    wqrot = _rotate_half(uq_pe).reshape(-1, MLA_HEADS * QK_ROPE_DIM).T
    ukv = w_ukv[i].astype(_BF16).reshape(-1, MLA_HEADS, QK_NOPE_DIM + V_HEAD_DIM)
    wk = jnp.pad(ukv[:, :, :QK_NOPE_DIM], ((0, 0), (0, 0), (0, HEAD_SLAB - QK_NOPE_DIM))).reshape(
        -1, MLA_HEADS * HEAD_SLAB)
    wv = ukv[:, :, QK_NOPE_DIM:].reshape(-1, MLA_HEADS * V_HEAD_DIM).T
    wwa = jnp.concatenate(
        [jnp.concatenate([w2[i], jnp.zeros_like(w2[i])], axis=1),
         jnp.concatenate([jnp.zeros_like(a2[i]), a2[i]], axis=1)], axis=0).astype(_BF16)
    stage_a = (row(g_mix), win, row(g_q_a), wq, wqrot, row(g_kv_a), wk, wv, row(mu_rwkv), row(w0),
               wwa, row(a0), g2[i].astype(_BF16), row(k_k), row(k_a))
    wkv = (r_k[i].reshape(1, -1), row(ln_x_w), row(ln_x_b))
    stage_d = (row(g_mix), wgate, w_o_mla[i].astype(_BF16), w_o_rwkv[i].astype(_BF16),
               w_out[i].astype(_BF16), row(g_ffn), w_ffn_up[i].astype(_BF16),
               w_ffn_down[i].astype(_BF16), row(g_ple), w_ple_gate[i].astype(_BF16),
               w_ple_proj[i].astype(_BF16))
    return stage_a, wkv, stage_d


def kernel(x, p, positions, g_mix, w_in, g_q_a, w_uq, g_kv_a, w_ukv, w_o_mla, mu_rwkv, w0, w2, a0, a2, g2, k_k, k_a, r_k, ln_x_w, ln_x_b, w_o_rwkv, w_out, g_ffn, w_ffn_up, w_ffn_down, g_ple, w_ple_gate, w_ple_proj, g_final):
    b, s, d = x.shape
    depth = w_in.shape[0]
    t = b * s
    cos_t, sin_t = _rope_tables(positions)
    gfin = g_final.reshape(1, -1)
    for i in range(depth):
        sa, sw, sd = _prepare_layer(i, g_mix, w_in, g_q_a, w_uq, g_kv_a, w_ukv, w_o_mla, mu_rwkv,
                                    w0, w2, a0, a2, g2, k_k, k_a, r_k, ln_x_w, ln_x_b, w_o_rwkv,
                                    w_out, g_ffn, w_ffn_up, w_ffn_down, g_ple, w_ple_gate,
                                    w_ple_proj)
        q, k, v, r, ld, kmod, vr, kkn, bvec, g = _stage_a(x, cos_t, sin_t, *sa)
        o_a = _attention(q, k, v)
        o_b = _wkv(r, ld, kmod, vr, kkn, bvec, g, *sw)
        out = _stage_d(x.reshape(t, d), o_a.reshape(t, -1), o_b.reshape(t, -1), p[i].reshape(t, -1),
                       *sd, gfin, final_norm=(i == depth - 1))
        x = out.reshape(b, s, d)
    return x
```
